```python
import jax, jax.numpy as jnp
from jax import lax
import numpy as np

D_MODEL = 1024
BATCH = 8
SEQ = 2048
DEPTH = 4
DEC_BATCH = 128
DEC_SEQ = 4
PAST_LEN = 16384
PAGE_SIZE = 128

N_MIXERS = 2
N_POOL = (DEPTH + 1) // 2
N_DN = DEPTH // 2
POOL_WINDOWS = (2, 4, 8, 16)
N_POOL_GROUPS = len(POOL_WINDOWS)
POOL_GROUP = D_MODEL // N_POOL_GROUPS
POOL_MAX = max(POOL_WINDOWS)
DK = 128
DV = 128
HK = D_MODEL // 128
HV = 2 * HK
KEY_DIM = HK * DK
VAL_DIM = HV * DV
CONV_W = 4
CONV_DIM = 2 * KEY_DIM + VAL_DIM
IN_DIM = CONV_DIM + VAL_DIM + 2 * HV
CHUNK = 64
D_FF = 2816
EPS = 1e-6

kernel_name = 'hybrid_pool_deltanet_macaron_step'


def rmsnorm(x, g):
    xf = x.astype(jnp.float32)
    y = xf * lax.rsqrt(jnp.mean(xf * xf, axis=-1, keepdims=True) + EPS) * g.astype(jnp.float32)
    return y.astype(x.dtype)


def l2norm(x):
    return x * lax.rsqrt(jnp.sum(x * x, axis=-1, keepdims=True) + EPS)


def swiglu(h, w_gate, w_up, w_down):
    return (jax.nn.silu(h @ w_gate) * (h @ w_up)) @ w_down


def pool_mixer(h, past, pos0, w_group, scale):
    B, S, D = h.shape
    P = POOL_MAX - 1
    hp = jnp.concatenate([past.astype(h.dtype), h], axis=1)
    cs = jnp.cumsum(hp.astype(jnp.float32), axis=1)
    cs = jnp.pad(cs, ((0, 0), (1, 0), (0, 0)))
    pos = pos0 + jnp.arange(S)
    groups = []
    for gi, w in enumerate(POOL_WINDOWS):
        lo, hi = gi * POOL_GROUP, (gi + 1) * POOL_GROUP
        win = cs[:, P + 1:P + 1 + S, lo:hi] - cs[:, P + 1 - w:P + 1 - w + S, lo:hi]
        cnt = jnp.minimum(pos + 1, w).astype(jnp.float32)[None, :, None]
        groups.append(win / cnt)
    pooled = jnp.concatenate(groups, axis=-1) - h.astype(jnp.float32)
    pooled = pooled.reshape(B, S, N_POOL_GROUPS, POOL_GROUP)
    y = jnp.einsum('bsgc,gcd->bsgd', pooled, w_group.astype(jnp.float32)).reshape(B, S, D)
    y = y * scale.astype(jnp.float32)
    return y.astype(h.dtype), hp[:, -P:]


def short_conv(u, past, w):
    S = u.shape[1]
    up = jnp.concatenate([past.astype(u.dtype), u], axis=1)
    out = up[:, 0:S] * w[0]
    for j in range(1, CONV_W):
        out = out + up[:, j:j + S] * w[j]
    return jax.nn.silu(out), up[:, -(CONV_W - 1):]


def chunk_gated_delta(q, k, v, beta, g, s0):
    B, S, H, _ = q.shape
    C = CHUNK if S >= CHUNK else S
    pad = (-S) % C
    if pad:
        q, k, v = [jnp.pad(t, ((0, 0), (0, pad), (0, 0), (0, 0))) for t in (q, k, v)]
        beta, g = [jnp.pad(t, ((0, 0), (0, pad), (0, 0))) for t in (beta, g)]
    N = (S + pad) // C

    def chunks(t):
        t = jnp.moveaxis(t, 2, 1)
        return t.reshape((B, H, N, C) + t.shape[3:])

    q, k, v, beta, g = [chunks(t) for t in (q, k, v, beta, g)]
    gc = jnp.cumsum(g, axis=-1)
    causal = jnp.tril(jnp.ones((C, C), dtype=bool))
    strict = jnp.tril(jnp.ones((C, C), dtype=bool), -1)
    decay = jnp.exp(jnp.where(causal, gc[..., :, None] - gc[..., None, :], -jnp.inf))
    kb = k * beta[..., None]
    vb = v * beta[..., None]
    lower = jnp.where(strict, jnp.einsum('bhnid,bhnjd->bhnij', kb, k) * decay, 0.0)
    eye = jnp.eye(C, dtype=jnp.float32)
    a_mat = eye + lower
    t_mat = lax.linalg.triangular_solve(a_mat, jnp.broadcast_to(eye, a_mat.shape),
                                        left_side=True, lower=True, unit_diagonal=True)
    u = jnp.einsum('bhnij,bhnje->bhnie', t_mat, vb)
    w = jnp.einsum('bhnij,bhnjd->bhnid', t_mat, kb * jnp.exp(gc)[..., None])
    attn = jnp.einsum('bhnid,bhnjd->bhnij', q, k) * decay
    xs = [jnp.moveaxis(t, 2, 0) for t in (q, k, u, w, gc, attn)]

    def step(state, inp):
        qi, ki, ui, wi, gi, ai = inp
        v_new = ui - jnp.einsum('bhcd,bhde->bhce', wi, state)
        o = (jnp.einsum('bhcd,bhde->bhce', qi * jnp.exp(gi)[..., None], state)
             + jnp.einsum('bhcj,bhje->bhce', ai, v_new))
        g_last = gi[..., -1]
        k_dec = ki * jnp.exp(g_last[..., None] - gi)[..., None]
        state = state * jnp.exp(g_last)[..., None, None] + jnp.einsum('bhcd,bhce->bhde', k_dec, v_new)
        return state, o

    s_final, o = lax.scan(step, s0, xs)
    o = jnp.moveaxis(o, 0, 2).reshape(B, H, N * C, -1)
    o = jnp.moveaxis(o, 1, 2)[:, :S]
    return o, s_final


def gated_deltanet(h, conv_past, s0, w_in, conv_w, a_log, dt_bias, norm_w, w_out):
    B, S, _ = h.shape
    proj = h @ w_in
    qkv_raw = proj[..., :CONV_DIM]
    z = proj[..., CONV_DIM:CONV_DIM + VAL_DIM]
    b = proj[..., CONV_DIM + VAL_DIM:CONV_DIM + VAL_DIM + HV]
    a = proj[..., CONV_DIM + VAL_DIM + HV:]
    qkv, new_conv = short_conv(qkv_raw, conv_past, conv_w)
    qkv = qkv.astype(jnp.float32)
    q = l2norm(qkv[..., :KEY_DIM].reshape(B, S, HK, DK))
    k = l2norm(qkv[..., KEY_DIM:2 * KEY_DIM].reshape(B, S, HK, DK))
    v = qkv[..., 2 * KEY_DIM:].reshape(B, S, HV, DV)
    q = jnp.repeat(q, HV // HK, axis=2) * (DK ** -0.5)
    k = jnp.repeat(k, HV // HK, axis=2)
    beta = jax.nn.sigmoid(b.astype(jnp.float32))
    g = -jnp.exp(a_log.astype(jnp.float32)) * jax.nn.softplus(a.astype(jnp.float32) + dt_bias.astype(jnp.float32))
    o, s_new = chunk_gated_delta(q, k, v, beta, g, s0.astype(jnp.float32))
    o = rmsnorm(o, norm_w) * jax.nn.silu(z.astype(jnp.float32).reshape(B, S, HV, DV))
    y = o.reshape(B, S, VAL_DIM).astype(h.dtype) @ w_out
    return y, new_conv, s_new.astype(s0.dtype)


def trunk(x, st_pool, st_conv, st_delta, pos0, p):
    new_pool, new_conv, new_delta = [], [], []
    for i in range(DEPTH):
        ng = p['norm_gains'][i]
        x = x + 0.5 * rmsnorm(swiglu(rmsnorm(x, ng[0]), p['w_ffn_gate'][i, 0],
                                     p['w_ffn_up'][i, 0], p['w_ffn_down'][i, 0]), ng[1])
        h = rmsnorm(x, ng[2])
        j = i // N_MIXERS
        if i % N_MIXERS == 0:
            y, ps = pool_mixer(h, st_pool[j], pos0, p['pool_w'][j], p['pool_scale'][j])
            new_pool.append(ps)
        else:
            y, cs, ds = gated_deltanet(h, st_conv[j], st_delta[j], p['dn_w_in'][j], p['dn_conv_w'][j],
                                       p['dn_a_log'][j], p['dn_dt_bias'][j], p['dn_norm_w'][j],
                                       p['dn_w_out'][j])
            new_conv.append(cs)
            new_delta.append(ds)
        x = x + rmsnorm(y, ng[3])
        x = x + 0.5 * rmsnorm(swiglu(rmsnorm(x, ng[4]), p['w_ffn_gate'][i, 1],
                                     p['w_ffn_up'][i, 1], p['w_ffn_down'][i, 1]), ng[5])
    return x, jnp.stack(new_pool), jnp.stack(new_conv), jnp.stack(new_delta)


def setup_inputs(seed: int = 0) -> dict:
    key = jax.random.key(seed)
    ks = jax.random.split(key, 20)
    f32 = jnp.float32
    nrm = lambda k, shape, s: jax.random.normal(k, shape, f32) * s
    x_prompt = nrm(ks[0], (BATCH, SEQ, D_MODEL), 1.0)
    x_sample = nrm(ks[1], (DEC_BATCH, DEC_SEQ, D_MODEL), 1.0)
    state_pool = nrm(ks[2], (N_POOL, DEC_BATCH, POOL_MAX - 1, D_MODEL), 1.0)
    state_conv = nrm(ks[3], (N_DN, DEC_BATCH, CONV_W - 1, CONV_DIM), 1.0)
    state_delta = nrm(ks[4], (N_DN, DEC_BATCH, HV, DK, DV), 0.05)
    norm_gains = 1.0 + nrm(ks[5], (DEPTH, 6, D_MODEL), 0.02)
    w_ffn_gate = nrm(ks[6], (DEPTH, 2, D_MODEL, D_FF), D_MODEL ** -0.5)
    w_ffn_up = nrm(ks[7], (DEPTH, 2, D_MODEL, D_FF), D_MODEL ** -0.5)
    w_ffn_down = nrm(ks[8], (DEPTH, 2, D_FF, D_MODEL), D_FF ** -0.5)
    pool_w = nrm(ks[9], (N_POOL, N_POOL_GROUPS, POOL_GROUP, POOL_GROUP), POOL_GROUP ** -0.5)
    pool_scale = 1.0 + nrm(ks[10], (N_POOL, D_MODEL), 0.1)
    dn_w_in = nrm(ks[11], (N_DN, D_MODEL, IN_DIM), D_MODEL ** -0.5)
    dn_conv_w = nrm(ks[12], (N_DN, CONV_W, CONV_DIM), CONV_W ** -0.5)
    dn_a_log = jnp.log(jax.random.uniform(ks[13], (N_DN, HV), f32, 1.0, 16.0))
    dt = jnp.exp(jax.random.uniform(ks[14], (N_DN, HV), f32, np.log(1e-3), np.log(1e-1)))
    dn_dt_bias = dt + jnp.log(-jnp.expm1(-dt))
    dn_norm_w = 1.0 + nrm(ks[15], (N_DN, DV), 0.02)
    dn_w_out = nrm(ks[16], (N_DN, VAL_DIM, D_MODEL), VAL_DIM ** -0.5)
    return {'x_prompt': x_prompt, 'x_sample': x_sample, 'state_pool': state_pool,
            'state_conv': state_conv, 'state_delta': state_delta, 'norm_gains': norm_gains,
            'w_ffn_gate': w_ffn_gate, 'w_ffn_up': w_ffn_up, 'w_ffn_down': w_ffn_down,
            'pool_w': pool_w, 'pool_scale': pool_scale, 'dn_w_in': dn_w_in, 'dn_conv_w': dn_conv_w,
            'dn_a_log': dn_a_log, 'dn_dt_bias': dn_dt_bias, 'dn_norm_w': dn_norm_w, 'dn_w_out': dn_w_out}


def reference(x_prompt, x_sample, state_pool, state_conv, state_delta, norm_gains, w_ffn_gate,
              w_ffn_up, w_ffn_down, pool_w, pool_scale, dn_w_in, dn_conv_w, dn_a_log, dn_dt_bias,
              dn_norm_w, dn_w_out):
    p = {'norm_gains': norm_gains, 'w_ffn_gate': w_ffn_gate, 'w_ffn_up': w_ffn_up,
         'w_ffn_down': w_ffn_down, 'pool_w': pool_w, 'pool_scale': pool_scale, 'dn_w_in': dn_w_in,
         'dn_conv_w': dn_conv_w, 'dn_a_log': dn_a_log, 'dn_dt_bias': dn_dt_bias,
         'dn_norm_w': dn_norm_w, 'dn_w_out': dn_w_out}
    zp = jnp.zeros((N_POOL, BATCH, POOL_MAX - 1, D_MODEL), x_prompt.dtype)
    zc = jnp.zeros((N_DN, BATCH, CONV_W - 1, CONV_DIM), x_prompt.dtype)
    zd = jnp.zeros((N_DN, BATCH, HV, DK, DV), state_delta.dtype)
    y_prompt, pool_p, conv_p, delta_p = trunk(x_prompt, zp, zc, zd, 0, p)
    y_sample, pool_s, conv_s, delta_s = trunk(x_sample, state_pool, state_conv, state_delta, PAST_LEN, p)
    return (y_prompt, y_sample, pool_p, conv_p, delta_p, pool_s, conv_s, delta_s)
```

```python
import functools

import jax
import jax.numpy as jnp
from jax import lax
from jax.experimental import pallas as pl
from jax.experimental.pallas import tpu as pltpu

F32 = jnp.float32
BF16 = jnp.bfloat16

EPS = 1e-6
POOL_WINDOWS = (2, 4, 8, 16)
POOL_PAST = max(POOL_WINDOWS) - 1
HEAD_DIM = 128
CONV_W = 4
CONV_PAST = CONV_W - 1
CHUNK = 64
MIN_CHUNK = 8
INV_BASE = 16

VMEM_LIMIT = 56 * 1024 * 1024
TOKEN_TILE = 512


def _params(*sem):
    return pltpu.CompilerParams(dimension_semantics=sem, vmem_limit_bytes=VMEM_LIMIT)


def _resident(shape):
    nd = len(shape)
    return pl.BlockSpec(shape, lambda *_: (0,) * nd, pipeline_mode=pl.Buffered(1))


def _rms(x, g):
    return x * lax.rsqrt(jnp.mean(x * x, axis=-1, keepdims=True) + EPS) * g


def _silu(x):
    return x / (1.0 + jnp.exp(-x))


def _dot(a, b):
    return jnp.dot(a.astype(BF16), b.astype(BF16), preferred_element_type=F32)


def _dot_nt(a, b):
    return lax.dot_general(a.astype(BF16), b.astype(BF16), (((1,), (1,)), ((), ())),
                           preferred_element_type=F32)


def _dot_tn(a, b):
    return lax.dot_general(a.astype(BF16), b.astype(BF16), (((0,), (0,)), ((), ())),
                           preferred_element_type=F32)


def _dot_f32(a, b):
    return jnp.dot(a, b, preferred_element_type=F32, precision=lax.Precision.HIGHEST)


def _ffn_kernel(x_ref, ng_ref, wg_ref, wu_ref, wd_ref, o_ref, *, pre, post, tf):
    x = x_ref[...]
    h = _rms(x, ng_ref[pre:pre + 1, :]).astype(BF16)
    acc = jnp.zeros(x.shape, F32)
    for c in range(wg_ref.shape[1] // tf):
        sl = slice(c * tf, (c + 1) * tf)
        gate = jnp.dot(h, wg_ref[:, sl], preferred_element_type=F32)
        up = jnp.dot(h, wu_ref[:, sl], preferred_element_type=F32)
        acc = acc + _dot(_silu(gate) * up, wd_ref[sl, :])
    o_ref[...] = x + 0.5 * _rms(acc, ng_ref[post:post + 1, :])


def _ffn(x, ng, wg, wu, wd, pre, post):
    t, d = x.shape
    tm = min(TOKEN_TILE, t)
    kern = functools.partial(_ffn_kernel, pre=pre, post=post, tf=256)
    return pl.pallas_call(
        kern,
        grid=(t // tm,),
        in_specs=[pl.BlockSpec((tm, d), lambda i: (i, 0)), _resident(ng.shape),
                  _resident(wg.shape), _resident(wu.shape), _resident(wd.shape)],
        out_specs=pl.BlockSpec((tm, d), lambda i: (i, 0)),
        out_shape=jax.ShapeDtypeStruct((t, d), F32),
        compiler_params=_params("parallel"),
        name="ffn",
    )(x, ng, wg, wu, wd)


def _pool_seq_kernel(x_ref, past_ref, ng_ref, w_ref, sc_ref, o_ref, np_ref, hp_ref,
                     *, ts, pos0, pre, post):
    s = pl.program_id(1)
    off = POOL_PAST + 1
    d = x_ref.shape[-1]
    gw = d // len(POOL_WINDOWS)

    @pl.when(s == 0)
    def _():
        hp_ref[0:1, :] = jnp.zeros((1, d), F32)
        hp_ref[1:off, :] = past_ref[0]

    x = x_ref[0]
    h = _rms(x, ng_ref[pre:pre + 1, :])
    hp_ref[off:off + ts, :] = h
    pos = pos0 + s * ts + lax.broadcasted_iota(jnp.int32, (ts, 1), 0)
    outs = []
    for gi, w in enumerate(POOL_WINDOWS):
        lo, hi = gi * gw, (gi + 1) * gw
        hg = h[:, lo:hi]
        win = hg
        for k in range(1, w):
            win = win + hp_ref[off - k:off - k + ts, lo:hi]
        cnt = jnp.minimum(pos + 1, w).astype(F32)
        outs.append(_dot(win / cnt - hg, w_ref[gi]))
    y = jnp.concatenate(outs, axis=-1) * sc_ref[...]
    o_ref[0] = x + _rms(y, ng_ref[post:post + 1, :])
    tail = hp_ref[ts + 1:ts + off, :]
    np_ref[0] = tail
    hp_ref[1:off, :] = tail


def _pool_seq(x, past, ng, w, sc, pos0, pre, post):
    b, s, d = x.shape
    ts = min(TOKEN_TILE, s)
    kern = functools.partial(_pool_seq_kernel, ts=ts, pos0=pos0, pre=pre, post=post)
    return pl.pallas_call(
        kern,
        grid=(b, s // ts),
        in_specs=[pl.BlockSpec((1, ts, d), lambda i, j: (i, j, 0)),
                  pl.BlockSpec((1, POOL_PAST, d), lambda i, j: (i, 0, 0)),
                  _resident(ng.shape), _resident(w.shape), _resident(sc.shape)],
        out_specs=[pl.BlockSpec((1, ts, d), lambda i, j: (i, j, 0)),
                   pl.BlockSpec((1, POOL_PAST, d), lambda i, j: (i, 0, 0))],
        out_shape=[jax.ShapeDtypeStruct((b, s, d), F32),
                   jax.ShapeDtypeStruct((b, POOL_PAST, d), F32)],
        scratch_shapes=[pltpu.VMEM((POOL_PAST + 1 + ts, d), F32)],
        compiler_params=_params("parallel", "arbitrary"),
        name="pool_seq",
    )(x, past, ng, w, sc)


def _pool_step_kernel(x_ref, past_ref, ng_ref, w_ref, sc_ref, o_ref, np_ref, *, pos0, pre, post):
    steps, bb, d = x_ref.shape
    gw = d // len(POOL_WINDOWS)
    xs = [x_ref[t] for t in range(steps)]
    hs = [_rms(x, ng_ref[pre:pre + 1, :]) for x in xs]
    hp = [past_ref[p] for p in range(POOL_PAST)] + hs
    outs = []
    for gi, w in enumerate(POOL_WINDOWS):
        lo, hi = gi * gw, (gi + 1) * gw
        rows = []
        for t in range(steps):
            win = hp[POOL_PAST + t][:, lo:hi]
            for k in range(1, w):
                win = win + hp[POOL_PAST + t - k][:, lo:hi]
            cnt = float(min(pos0 + t + 1, w))
            rows.append(win / cnt - hs[t][:, lo:hi])
        outs.append(_dot(jnp.concatenate(rows, axis=0), w_ref[gi]))
    y = jnp.concatenate(outs, axis=-1) * sc_ref[...]
    for t in range(steps):
        o_ref[t] = xs[t] + _rms(y[t * bb:(t + 1) * bb], ng_ref[post:post + 1, :])
    for p in range(POOL_PAST):
        np_ref[p] = hp[steps + p]


def _pool_step(x, past, ng, w, sc, pos0, pre, post):
    steps, b, d = x.shape
    bb = min(32, b)
    kern = functools.partial(_pool_step_kernel, pos0=pos0, pre=pre, post=post)
    return pl.pallas_call(
        kern,
        grid=(b // bb,),
        in_specs=[pl.BlockSpec((steps, bb, d), lambda i: (0, i, 0)),
                  pl.BlockSpec((POOL_PAST, bb, d), lambda i: (0, i, 0)),
                  _resident(ng.shape), _resident(w.shape), _resident(sc.shape)],
        out_specs=[pl.BlockSpec((steps, bb, d), lambda i: (0, i, 0)),
                   pl.BlockSpec((POOL_PAST, bb, d), lambda i: (0, i, 0))],
        out_shape=[jax.ShapeDtypeStruct((steps, b, d), F32),
                   jax.ShapeDtypeStruct((POOL_PAST, b, d), F32)],
        compiler_params=_params("parallel"),
        name="pool_step",
    )(x, past, ng, w, sc)


def _dn_proj_kernel(x_ref, ng_ref, w_ref, wb_ref, wa_ref, alog_ref, dtb_ref,
                    qkv_ref, z_ref, beta_ref, g_ref, *, pre, tn):
    h = _rms(x_ref[...], ng_ref[pre:pre + 1, :]).astype(BF16)
    n_qkv = qkv_ref.shape[1]
    for c in range(n_qkv // tn):
        sl = slice(c * tn, (c + 1) * tn)
        qkv_ref[:, sl] = jnp.dot(h, w_ref[:, sl], preferred_element_type=F32)
    for c in range(z_ref.shape[1] // tn):
        z_ref[:, c * tn:(c + 1) * tn] = jnp.dot(
            h, w_ref[:, n_qkv + c * tn:n_qkv + (c + 1) * tn], preferred_element_type=F32)
    b = jnp.dot(h, wb_ref[...], preferred_element_type=F32)
    a = jnp.dot(h, wa_ref[...], preferred_element_type=F32) + dtb_ref[...]
    beta_ref[...] = 1.0 / (1.0 + jnp.exp(-b))
    softplus = jnp.maximum(a, 0.0) + jnp.log1p(jnp.exp(-jnp.abs(a)))
    g_ref[...] = -jnp.exp(alog_ref[...]) * softplus


def _dn_proj(x, ng, w_main, w_b, w_a, a_log, dt_bias, conv_dim, val_dim, pre):
    t, d = x.shape
    tm = min(256, t)
    hv = w_b.shape[1]
    kern = functools.partial(_dn_proj_kernel, pre=pre, tn=512)
    row = lambda n: pl.BlockSpec((tm, n), lambda i: (i, 0))
    return pl.pallas_call(
        kern,
        grid=(t // tm,),
        in_specs=[row(d), _resident(ng.shape), _resident(w_main.shape), _resident(w_b.shape),
                  _resident(w_a.shape), _resident(a_log.shape), _resident(dt_bias.shape)],
        out_specs=[row(conv_dim), row(val_dim), row(hv), row(hv)],
        out_shape=[jax.ShapeDtypeStruct((t, conv_dim), F32), jax.ShapeDtypeStruct((t, val_dim), F32),
                   jax.ShapeDtypeStruct((t, hv), F32), jax.ShapeDtypeStruct((t, hv), F32)],
        compiler_params=_params("parallel"),
        name="dn_proj",
    )(x, ng, w_main, w_b, w_a, a_log, dt_bias)


def _unit_lower_inverse(l_mat, c):
    row = lax.broadcasted_iota(jnp.int32, (c, c), 0)
    col = lax.broadcasted_iota(jnp.int32, (c, c), 1)
    blk = jnp.bitwise_xor(row, col)
    bs = min(INV_BASE, c)
    a = jnp.where(blk < bs, -l_mat, 0.0)
    t = jnp.where(row == col, 1.0, 0.0) + a
    p = a
    n = 2
    while n < bs:
        p = _dot_f32(p, p)
        t = t + _dot_f32(t, p)
        n *= 2
    while bs < c:
        e = jnp.where((blk < 2 * bs) & (blk >= bs), l_mat, 0.0)
        t = t - _dot_f32(_dot_f32(t, e), t)
        bs *= 2
    return t


def _dn_core_kernel(q_ref, k_ref, v_ref, z_ref, beta_ref, g_ref, cq_ref, ck_ref, cv_ref,
                    wq_ref, wk_ref, wv_ref, s0_ref, nw_ref,
                    o_ref, ncq_ref, nck_ref, ncv_ref, sn_ref,
                    bq_ref, bk_ref, bv_ref, s_ref, *, c, s_valid):
    hk = pl.program_id(1)
    bb, s_pad, dk = q_ref.shape
    hv = beta_ref.shape[-1]
    base = 8
    row = lax.broadcasted_iota(jnp.int32, (c, c), 0)
    col = lax.broadcasted_iota(jnp.int32, (c, c), 1)
    rid = lax.broadcasted_iota(jnp.int32, (c, 1), 0)
    head_lane = lax.broadcasted_iota(jnp.int32, (1, hv), 1)

    def conv(buf_ref, w_ref):
        out = buf_ref[base - CONV_PAST:base - CONV_PAST + c, :] * w_ref[0:1, :]
        for j in range(1, CONV_W):
            lo = base - CONV_PAST + j
            out = out + buf_ref[lo:lo + c, :] * w_ref[j:j + 1, :]
        return _silu(out)

    def per_seq(i, carry):
        for buf_ref, past_ref in ((bq_ref, cq_ref), (bk_ref, ck_ref), (bv_ref, cv_ref)):
            buf_ref[0:base - CONV_PAST, :] = jnp.zeros((base - CONV_PAST, buf_ref.shape[1]), F32)
            buf_ref[base - CONV_PAST:base, :] = past_ref[i]
        s_ref[...] = s0_ref[i]

        def chunk(ci, carry2):
            r0 = pl.multiple_of(ci * c, c)
            rows = pl.ds(r0, c)
            valid = rid < (s_valid - ci * c)
            convd = []
            for buf_ref, src_ref, w_ref in ((bq_ref, q_ref, wq_ref), (bk_ref, k_ref, wk_ref),
                                            (bv_ref, v_ref, wv_ref)):
                buf_ref[base:base + c, :] = src_ref[i, rows, :]
                convd.append(jnp.where(valid, conv(buf_ref, w_ref), 0.0))
                tail = buf_ref[base + c - CONV_PAST:base + c, :]
                buf_ref[base - CONV_PAST:base, :] = tail
            qc, kc, v = convd
            q = qc * lax.rsqrt(jnp.sum(qc * qc, axis=-1, keepdims=True) + EPS) * (dk ** -0.5)
            k = kc * lax.rsqrt(jnp.sum(kc * kc, axis=-1, keepdims=True) + EPS)
            kk = _dot_nt(k, k)
            qk = _dot_nt(q, k)
            beta_blk = beta_ref[i, rows, :]
            g_blk = g_ref[i, rows, :]
            for j in range(2):
                sel = head_lane == 2 * hk + j
                beta = jnp.where(valid, jnp.sum(jnp.where(sel, beta_blk, 0.0), axis=1, keepdims=True), 0.0)
                g = jnp.where(valid, jnp.sum(jnp.where(sel, g_blk, 0.0), axis=1, keepdims=True), 0.0)
                g_row = jnp.sum(jnp.where(row == col, g, 0.0), axis=0, keepdims=True)
                gc = jnp.sum(jnp.where(row >= col, g_row, 0.0), axis=1, keepdims=True)
                gc_row = jnp.sum(jnp.where(row <= col, g, 0.0), axis=0, keepdims=True)
                g_last = jnp.sum(g, axis=0, keepdims=True)
                decay = jnp.exp(jnp.where(row >= col, gc - gc_row, -jnp.inf))
                lower = jnp.where(row > col, kk * beta * decay, 0.0)
                t_mat = _unit_lower_inverse(lower, c)
                vj = v[:, j * dk:(j + 1) * dk]
                uw = _dot(t_mat, jnp.concatenate([vj * beta, k * (beta * jnp.exp(gc))], axis=1))
                u, w = uw[:, :dk], uw[:, dk:]
                state = s_ref[j]
                ws_qs = _dot(jnp.concatenate([w, q * jnp.exp(gc)], axis=0), state)
                v_new = u - ws_qs[:c]
                o = ws_qs[c:] + _dot(qk * decay, v_new)
                k_dec = k * jnp.exp(g_last - gc)
                s_ref[j] = state * jnp.exp(g_last) + _dot_tn(k_dec, v_new)
                zj = z_ref[i, rows, j * dk:(j + 1) * dk]
                o_ref[i, rows, j * dk:(j + 1) * dk] = _rms(o, nw_ref[...]) * _silu(zj)
            return carry2

        lax.fori_loop(0, s_pad // c, chunk, 0)
        sn_ref[i] = s_ref[...]
        ncq_ref[i] = q_ref[i, s_valid - CONV_PAST:s_valid, :]
        nck_ref[i] = k_ref[i, s_valid - CONV_PAST:s_valid, :]
        ncv_ref[i] = v_ref[i, s_valid - CONV_PAST:s_valid, :]
        return carry

    lax.fori_loop(0, bb, per_seq, 0)


def _dn_core(qkv, z, beta, g, conv_past, conv_w, s0, norm_w, c, s_valid, bb):
    b, s_pad, conv_dim = qkv.shape
    val_dim = z.shape[-1]
    hv = beta.shape[-1]
    dk = HEAD_DIM
    n_kh = hv // 2
    key_dim = n_kh * dk
    assert s_valid >= CONV_PAST and s_pad % c == 0 and conv_dim == 2 * key_dim + val_dim
    koff, voff = key_dim // dk, 2 * key_dim // (2 * dk)
    kern = functools.partial(_dn_core_kernel, c=c, s_valid=s_valid)
    seq = lambda n, off: pl.BlockSpec((bb, s_pad, n), lambda i, h: (i, 0, h + off))
    past = lambda n, off: pl.BlockSpec((bb, CONV_PAST, n), lambda i, h: (i, 0, h + off))
    wspec = lambda n, off: pl.BlockSpec((CONV_W, n), lambda i, h: (0, h + off))
    state = pl.BlockSpec((bb, 2, dk, dk), lambda i, h: (i, h, 0, 0))
    gate = pl.BlockSpec((bb, s_pad, hv), lambda i, h: (i, 0, 0))
    return pl.pallas_call(
        kern,
        grid=(b // bb, n_kh),
        in_specs=[seq(dk, 0), seq(dk, koff), seq(2 * dk, voff), seq(2 * dk, 0), gate, gate,
                  past(dk, 0), past(dk, koff), past(2 * dk, voff),
                  wspec(dk, 0), wspec(dk, koff), wspec(2 * dk, voff),
                  state, _resident(norm_w.shape)],
        out_specs=[seq(2 * dk, 0), past(dk, 0), past(dk, 0), past(2 * dk, 0), state],
        out_shape=[jax.ShapeDtypeStruct((b, s_pad, val_dim), F32),
                   jax.ShapeDtypeStruct((b, CONV_PAST, key_dim), F32),
                   jax.ShapeDtypeStruct((b, CONV_PAST, key_dim), F32),
                   jax.ShapeDtypeStruct((b, CONV_PAST, val_dim), F32),
                   jax.ShapeDtypeStruct(s0.shape, F32)],
        scratch_shapes=[pltpu.VMEM((8 + c, dk), F32), pltpu.VMEM((8 + c, dk), F32),
                        pltpu.VMEM((8 + c, 2 * dk), F32), pltpu.VMEM((2, dk, dk), F32)],
        compiler_params=_params("parallel", "parallel"),
        name="dn_core",
    )(qkv, qkv, qkv, z, beta, g, conv_past, conv_past, conv_past,
      conv_w, conv_w, conv_w, s0, norm_w)


def _dn_out_kernel(o_ref, x_ref, ng_ref, w_ref, y_ref, *, post):
    y = _dot(o_ref[...], w_ref[...])
    y_ref[...] = x_ref[...] + _rms(y, ng_ref[post:post + 1, :])


def _dn_out(o, x, ng, w_out, post):
    t, d = x.shape
    tm = min(TOKEN_TILE, t)
    kern = functools.partial(_dn_out_kernel, post=post)
    return pl.pallas_call(
        kern,
        grid=(t // tm,),
        in_specs=[pl.BlockSpec((tm, o.shape[1]), lambda i: (i, 0)),
                  pl.BlockSpec((tm, d), lambda i: (i, 0)),
                  _resident(ng.shape), _resident(w_out.shape)],
        out_specs=pl.BlockSpec((tm, d), lambda i: (i, 0)),
        out_shape=jax.ShapeDtypeStruct((t, d), F32),
        compiler_params=_params("parallel"),
        name="dn_out",
    )(o, x, ng, w_out)


def _dn_layer(x, conv_past, s0, ng, w_main, w_b, w_a, a_log, dt_bias, conv_w, norm_w, w_out,
              conv_dim, val_dim):
    b, s, d = x.shape
    x2 = x.reshape(b * s, d)
    qkv, z, beta, g = _dn_proj(x2, ng, w_main, w_b, w_a, a_log, dt_bias, conv_dim, val_dim, pre=2)
    c = CHUNK if s >= CHUNK else max(MIN_CHUNK, pl.next_power_of_2(s))
    s_pad = -(-s // c) * c
    shaped = [t.reshape(b, s, -1) for t in (qkv, z, beta, g)]
    if s_pad != s:
        shaped = [jnp.pad(t, ((0, 0), (0, s_pad - s), (0, 0))) for t in shaped]
    bb = 1 if s >= CHUNK else min(16, b)
    o, ncq, nck, ncv, s_new = _dn_core(*shaped, conv_past, conv_w, s0, norm_w, c, s, bb)
    y = _dn_out(o[:, :s].reshape(b * s, val_dim), x2, ng, w_out, post=3)
    return y.reshape(b, s, d), jnp.concatenate([ncq, nck, ncv], axis=-1), s_new


def _trunk(x, st_pool, st_conv, st_delta, pos0, p):
    b, s, d = x.shape
    depth = p["wg"].shape[0]
    new_pool, new_conv, new_delta = [], [], []
    for i in range(depth):
        ng = p["ng"][i]
        x = _ffn(x.reshape(b * s, d), ng, p["wg"][i, 0], p["wu"][i, 0], p["wd"][i, 0], 0, 1).reshape(b, s, d)
        j = i // 2
        if i % 2 == 0:
            if s >= 2 * POOL_PAST:
                x, ps = _pool_seq(x, st_pool[j], ng, p["pool_w"][j], p["pool_scale"][j], pos0, 2, 3)
            else:
                xt, pt = _pool_step(x.transpose(1, 0, 2), st_pool[j].transpose(1, 0, 2), ng,
                                    p["pool_w"][j], p["pool_scale"][j], pos0, 2, 3)
                x, ps = xt.transpose(1, 0, 2), pt.transpose(1, 0, 2)
            new_pool.append(ps)
        else:
            x, cs, ds = _dn_layer(x, st_conv[j], st_delta[j], ng, p["w_main"][j], p["w_b"][j],
                                  p["w_a"][j], p["a_log"][j], p["dt_bias"][j], p["conv_w"][j],
                                  p["norm_w"][j], p["w_out"][j], p["conv_dim"], p["val_dim"])
            new_conv.append(cs)
            new_delta.append(ds)
        x = _ffn(x.reshape(b * s, d), ng, p["wg"][i, 1], p["wu"][i, 1], p["wd"][i, 1], 4, 5).reshape(b, s, d)
    return x, jnp.stack(new_pool), jnp.stack(new_conv), jnp.stack(new_delta)


def kernel(x_prompt, x_sample, state_pool, state_conv, state_delta, norm_gains, w_ffn_gate,
           w_ffn_up, w_ffn_down, pool_w, pool_scale, dn_w_in, dn_conv_w, dn_a_log, dn_dt_bias,
           dn_norm_w, dn_w_out):
    n_dn, _, _, conv_dim = state_conv.shape
    hv = state_delta.shape[2]
    val_dim = hv * HEAD_DIM
    n_pool = state_pool.shape[0]
    d = x_prompt.shape[-1]
    p = {
        "ng": norm_gains,
        "wg": w_ffn_gate.astype(BF16), "wu": w_ffn_up.astype(BF16), "wd": w_ffn_down.astype(BF16),
        "pool_w": pool_w.astype(BF16), "pool_scale": pool_scale.reshape(n_pool, 1, d),
        "w_main": dn_w_in[:, :, :conv_dim + val_dim].astype(BF16),
        "w_b": dn_w_in[:, :, conv_dim + val_dim:conv_dim + val_dim + hv].astype(BF16),
        "w_a": dn_w_in[:, :, conv_dim + val_dim + hv:].astype(BF16),
        "a_log": dn_a_log.reshape(n_dn, 1, hv), "dt_bias": dn_dt_bias.reshape(n_dn, 1, hv),
        "conv_w": dn_conv_w, "norm_w": dn_norm_w.reshape(n_dn, 1, HEAD_DIM),
        "w_out": dn_w_out.astype(BF16), "conv_dim": conv_dim, "val_dim": val_dim,
    }
    bp = x_prompt.shape[0]
    zp = jnp.zeros((n_pool, bp) + state_pool.shape[2:], F32)
    zc = jnp.zeros((n_dn, bp) + state_conv.shape[2:], F32)
    zd = jnp.zeros((n_dn, bp) + state_delta.shape[2:], F32)
    past_len = 16384
    y_p, pool_p, conv_p, delta_p = _trunk(x_prompt, zp, zc, zd, 0, p)
    y_s, pool_s, conv_s, delta_s = _trunk(x_sample, state_pool, state_conv, state_delta, past_len, p)
    return (y_p, y_s, pool_p, conv_p, delta_p, pool_s, conv_s, delta_s)
```

```python
import functools

import jax
import jax.numpy as jnp
from jax import lax
from jax.experimental import pallas as pl
from jax.experimental.pallas import tpu as pltpu

F32 = jnp.float32
BF16 = jnp.bfloat16

EPS = 1e-6
PAST_LEN = 16384
POOL_WINDOWS = (2, 4, 8, 16)
POOL_PAST = max(POOL_WINDOWS) - 1
HEAD_DIM = 128
CONV_W = 4
CONV_PAST = CONV_W - 1
CHUNK = 64
SHORT_GROUP = 8
INV_BASE = 16
INTRA_UNITS = 4

VMEM_LIMIT = 56 * 1024 * 1024
TOKEN_TILE = 512


def _params(*sem):
    return pltpu.CompilerParams(dimension_semantics=sem, vmem_limit_bytes=VMEM_LIMIT)


def _resident(shape):
    nd = len(shape)
    return pl.BlockSpec(shape, lambda *_: (0,) * nd, pipeline_mode=pl.Buffered(1))


def _rms(x, g):
    return x * lax.rsqrt(jnp.mean(x * x, axis=-1, keepdims=True) + EPS) * g


def _silu(x):
    return x / (1.0 + jnp.exp(-x))


def _dot(a, b):
    return jnp.dot(a.astype(BF16), b.astype(BF16), preferred_element_type=F32)


def _dot_nt(a, b):
    return lax.dot_general(a.astype(BF16), b.astype(BF16), (((1,), (1,)), ((), ())),
                           preferred_element_type=F32)


def _ffn_kernel(x_ref, ng_ref, wg_ref, wu_ref, wd_ref, o_ref, *, pre, post, tf):
    x = x_ref[...]
    h = _rms(x, ng_ref[pre:pre + 1, :]).astype(BF16)
    acc = jnp.zeros(x.shape, F32)
    for c in range(wg_ref.shape[1] // tf):
        sl = slice(c * tf, (c + 1) * tf)
        gate = jnp.dot(h, wg_ref[:, sl], preferred_element_type=F32)
        up = jnp.dot(h, wu_ref[:, sl], preferred_element_type=F32)
        acc = acc + _dot(_silu(gate) * up, wd_ref[sl, :])
    o_ref[...] = x + 0.5 * _rms(acc, ng_ref[post:post + 1, :])


def _ffn(x, ng, wg, wu, wd, pre, post):
    t, d = x.shape
    tm = min(TOKEN_TILE, t)
    kern = functools.partial(_ffn_kernel, pre=pre, post=post, tf=256)
    return pl.pallas_call(
        kern,
        grid=(t // tm,),
        in_specs=[pl.BlockSpec((tm, d), lambda i: (i, 0)), _resident(ng.shape),
                  _resident(wg.shape), _resident(wu.shape), _resident(wd.shape)],
        out_specs=pl.BlockSpec((tm, d), lambda i: (i, 0)),
        out_shape=jax.ShapeDtypeStruct((t, d), F32),
        compiler_params=_params("parallel"),
        name="ffn",
    )(x, ng, wg, wu, wd)


def _pool_seq_kernel(x_ref, past_ref, ng_ref, w_ref, sc_ref, o_ref, np_ref, hp_ref,
                     *, ts, pos0, pre, post):
    s = pl.program_id(1)
    off = POOL_PAST + 1
    d = x_ref.shape[-1]
    gw = d // len(POOL_WINDOWS)

    @pl.when(s == 0)
    def _():
        hp_ref[0:1, :] = jnp.zeros((1, d), F32)
        hp_ref[1:off, :] = past_ref[0]

    x = x_ref[0]
    h = _rms(x, ng_ref[pre:pre + 1, :])
    hp_ref[off:off + ts, :] = h
    pos = pos0 + s * ts + lax.broadcasted_iota(jnp.int32, (ts, 1), 0)
    outs = []
    for gi, w in enumerate(POOL_WINDOWS):
        lo, hi = gi * gw, (gi + 1) * gw
        hg = h[:, lo:hi]
        win = hg
        for k in range(1, w):
            win = win + hp_ref[off - k:off - k + ts, lo:hi]
        cnt = jnp.minimum(pos + 1, w).astype(F32)
        outs.append(_dot(win / cnt - hg, w_ref[gi]))
    y = jnp.concatenate(outs, axis=-1) * sc_ref[...]
    o_ref[0] = x + _rms(y, ng_ref[post:post + 1, :])
    tail = hp_ref[ts + 1:ts + off, :]
    np_ref[0] = tail
    hp_ref[1:off, :] = tail


def _pool_seq(x, past, ng, w, sc, pos0, pre, post):
    b, s, d = x.shape
    ts = min(TOKEN_TILE, s)
    kern = functools.partial(_pool_seq_kernel, ts=ts, pos0=pos0, pre=pre, post=post)
    return pl.pallas_call(
        kern,
        grid=(b, s // ts),
        in_specs=[pl.BlockSpec((1, ts, d), lambda i, j: (i, j, 0)),
                  pl.BlockSpec((1, POOL_PAST, d), lambda i, j: (i, 0, 0)),
                  _resident(ng.shape), _resident(w.shape), _resident(sc.shape)],
        out_specs=[pl.BlockSpec((1, ts, d), lambda i, j: (i, j, 0)),
                   pl.BlockSpec((1, POOL_PAST, d), lambda i, j: (i, 0, 0))],
        out_shape=[jax.ShapeDtypeStruct((b, s, d), F32),
                   jax.ShapeDtypeStruct((b, POOL_PAST, d), F32)],
        scratch_shapes=[pltpu.VMEM((POOL_PAST + 1 + ts, d), F32)],
        compiler_params=_params("parallel", "arbitrary"),
        name="pool_seq",
    )(x, past, ng, w, sc)


def _pool_step_kernel(x_ref, past_ref, ng_ref, w_ref, sc_ref, o_ref, np_ref, *, pos0, pre, post):
    steps, bb, d = x_ref.shape
    gw = d // len(POOL_WINDOWS)
    xs = [x_ref[t] for t in range(steps)]
    hs = [_rms(x, ng_ref[pre:pre + 1, :]) for x in xs]
    hp = [past_ref[p] for p in range(POOL_PAST)] + hs
    outs = []
    for gi, w in enumerate(POOL_WINDOWS):
        lo, hi = gi * gw, (gi + 1) * gw
        rows = []
        for t in range(steps):
            win = hp[POOL_PAST + t][:, lo:hi]
            for k in range(1, w):
                win = win + hp[POOL_PAST + t - k][:, lo:hi]
            cnt = float(min(pos0 + t + 1, w))
            rows.append(win / cnt - hs[t][:, lo:hi])
        outs.append(_dot(jnp.concatenate(rows, axis=0), w_ref[gi]))
    y = jnp.concatenate(outs, axis=-1) * sc_ref[...]
    for t in range(steps):
        o_ref[t] = xs[t] + _rms(y[t * bb:(t + 1) * bb], ng_ref[post:post + 1, :])
    for p in range(POOL_PAST):
        np_ref[p] = hp[steps + p]


def _pool_step(x, past, ng, w, sc, pos0, pre, post):
    steps, b, d = x.shape
    bb = min(32, b)
    kern = functools.partial(_pool_step_kernel, pos0=pos0, pre=pre, post=post)
    return pl.pallas_call(
        kern,
        grid=(b // bb,),
        in_specs=[pl.BlockSpec((steps, bb, d), lambda i: (0, i, 0)),
                  pl.BlockSpec((POOL_PAST, bb, d), lambda i: (0, i, 0)),
                  _resident(ng.shape), _resident(w.shape), _resident(sc.shape)],
        out_specs=[pl.BlockSpec((steps, bb, d), lambda i: (0, i, 0)),
                   pl.BlockSpec((POOL_PAST, bb, d), lambda i: (0, i, 0))],
        out_shape=[jax.ShapeDtypeStruct((steps, b, d), F32),
                   jax.ShapeDtypeStruct((POOL_PAST, b, d), F32)],
        compiler_params=_params("parallel"),
        name="pool_step",
    )(x, past, ng, w, sc)


def _dn_proj_kernel(x_ref, ng_ref, w_ref, wb_ref, wa_ref, alog_ref, dtb_ref,
                    hm_ref, beta_ref, g_ref, *, pre, tn):
    h = _rms(x_ref[...], ng_ref[pre:pre + 1, :]).astype(BF16)
    per = tn // HEAD_DIM
    for c in range(w_ref.shape[1] // tn):
        res = jnp.dot(h, w_ref[:, c * tn:(c + 1) * tn], preferred_element_type=F32)
        for m in range(per):
            hm_ref[c * per + m] = res[:, m * HEAD_DIM:(m + 1) * HEAD_DIM]
    b = jnp.dot(h, wb_ref[...], preferred_element_type=F32)
    a = jnp.dot(h, wa_ref[...], preferred_element_type=F32) + dtb_ref[...]
    beta_ref[...] = 1.0 / (1.0 + jnp.exp(-b))
    softplus = jnp.maximum(a, 0.0) + jnp.log1p(jnp.exp(-jnp.abs(a)))
    g_ref[...] = -jnp.exp(alog_ref[...]) * softplus


def _dn_proj(x, ng, w_main, w_b, w_a, a_log, dt_bias, pre):
    t, d = x.shape
    tm = min(256, t)
    hv = w_b.shape[1]
    n_blk = w_main.shape[1] // HEAD_DIM
    kern = functools.partial(_dn_proj_kernel, pre=pre, tn=512)
    row = lambda n: pl.BlockSpec((tm, n), lambda i: (i, 0))
    return pl.pallas_call(
        kern,
        grid=(t // tm,),
        in_specs=[row(d), _resident(ng.shape), _resident(w_main.shape), _resident(w_b.shape),
                  _resident(w_a.shape), _resident(a_log.shape), _resident(dt_bias.shape)],
        out_specs=[pl.BlockSpec((n_blk, tm, HEAD_DIM), lambda i: (0, i, 0)), row(hv), row(hv)],
        out_shape=[jax.ShapeDtypeStruct((n_blk, t, HEAD_DIM), F32),
                   jax.ShapeDtypeStruct((t, hv), F32), jax.ShapeDtypeStruct((t, hv), F32)],
        compiler_params=_params("parallel"),
        name="dn_proj",
    )(x, ng, w_main, w_b, w_a, a_log, dt_bias)


def _dot3_each(pairs):
    split = lambda x: (x.astype(BF16), (x - x.astype(BF16).astype(F32)).astype(BF16))
    parts = [(split(a), split(b)) for a, b in pairs]
    mm = lambda x, y: jnp.dot(x, y, preferred_element_type=F32)
    hh = [mm(a[0], b[0]) for a, b in parts]
    hl = [mm(a[0], b[1]) for a, b in parts]
    lh = [mm(a[1], b[0]) for a, b in parts]
    return [x + (y + z) for x, y, z in zip(hh, hl, lh)]


def _unit_lower_inverse_each(l_mats, c):
    row = lax.broadcasted_iota(jnp.int32, (c, c), 0)
    col = lax.broadcasted_iota(jnp.int32, (c, c), 1)
    blk = jnp.bitwise_xor(row, col)
    eye = jnp.where(row == col, 1.0, 0.0)
    bs = min(INV_BASE, c)
    ps = [jnp.where(blk < bs, -l, 0.0) for l in l_mats]
    ts = [eye + p for p in ps]
    n = 2
    while n < bs:
        ps = [_dot(p, p) for p in ps]
        tps = [_dot(t, p) for t, p in zip(ts, ps)]
        ts = [t + tp for t, tp in zip(ts, tps)]
        n *= 2
    while bs < c:
        es = [jnp.where((blk < 2 * bs) & (blk >= bs), l, 0.0) for l in l_mats]
        tes = [_dot(t, e) for t, e in zip(ts, es)]
        tets = [_dot(te, t) for te, t in zip(tes, ts)]
        ts = [t - tet for t, tet in zip(ts, tets)]
        bs *= 2
    lts = _dot3_each(list(zip(l_mats, ts)))
    trs = [_dot(t, eye - t - lt) for t, lt in zip(ts, lts)]
    return [t + tr for t, tr in zip(ts, trs)]


def _dn_seq_kernel(hm_ref, beta_ref, g_ref, cp_ref, cw_ref, s0_ref, nw_ref,
                   o_ref, sn_ref,
                   raw_ref, u_ref, wq_ref, attn_ref, kdt_ref, egl_ref, *, c, bb, n_kh):
    step = pl.program_id(1)
    dk = HEAD_DIM
    hv = 2 * n_kh
    n_conv = 4 * n_kh
    rows_per_step = hm_ref.shape[1]
    n_chunks = rows_per_step // c
    halo = CONV_PAST * bb
    off = raw_ref.shape[1] - rows_per_step

    @pl.when(step == 0)
    def _():
        raw_ref[:, off - halo:off, :] = cp_ref[0]
        sn_ref[...] = s0_ref[...]

    @pl.when(step > 0)
    def _():
        raw_ref[:, off - halo:off, :] = raw_ref[:, off + rows_per_step - halo:off + rows_per_step, :]

    raw_ref[:, off:off + rows_per_step, :] = hm_ref[0:n_conv]

    row = lax.broadcasted_iota(jnp.int32, (c, c), 0)
    col = lax.broadcasted_iota(jnp.int32, (c, c), 1)
    same = (jnp.bitwise_xor(row, col) & (bb - 1)) == 0
    low_eq = same & (row >= col)
    low = same & (row > col)
    up_eq = same & (row <= col)
    last = same & (col >= c - bb)
    eye = row == col
    head_lane = lax.broadcasted_iota(jnp.int32, (1, hv), 1)
    member = lax.broadcasted_iota(jnp.int32, (c, 1), 0) & (bb - 1)

    def conv(cb, r0):
        acc = None
        for j in range(CONV_W):
            lo = off + r0 - (CONV_PAST - j) * bb
            term = raw_ref[cb, lo:lo + c, :] * cw_ref[cb, j:j + 1, :]
            acc = term if acc is None else acc + term
        return _silu(acc)


    def intra(units):
        qs, ks = [], []
        for hk, ci in units:
            qc = conv(hk, ci * c)
            kc = conv(n_kh + hk, ci * c)
            qs.append(qc * lax.rsqrt(jnp.sum(qc * qc, axis=-1, keepdims=True) + EPS) * (dk ** -0.5))
            ks.append(kc * lax.rsqrt(jnp.sum(kc * kc, axis=-1, keepdims=True) + EPS))
        kks = [_dot_nt(k, k) for k in ks]
        qks = [_dot_nt(q, k) for q, k in zip(qs, ks)]
        probs = []
        for (hk, ci), q, k, kk, qk in zip(units, qs, ks, kks, qks):
            r0 = ci * c
            beta_blk = beta_ref[r0:r0 + c, :]
            g_blk = g_ref[r0:r0 + c, :]
            for j in range(2):
                h = 2 * hk + j
                sel = head_lane == h
                beta = jnp.sum(jnp.where(sel, beta_blk, 0.0), axis=1, keepdims=True)
                g = jnp.sum(jnp.where(sel, g_blk, 0.0), axis=1, keepdims=True)
                g_row = jnp.sum(jnp.where(eye, g, 0.0), axis=0, keepdims=True)
                gc = jnp.sum(jnp.where(low_eq, g_row, 0.0), axis=1, keepdims=True)
                gc_row = jnp.sum(jnp.where(up_eq, g, 0.0), axis=0, keepdims=True)
                g_last = jnp.sum(jnp.where(last, gc_row, 0.0), axis=1, keepdims=True)
                decay = jnp.exp(jnp.where(low_eq, gc - gc_row, -jnp.inf))
                egc = jnp.exp(gc)
                attn_ref[h, r0:r0 + c, :] = (qk * decay).astype(BF16)
                wq_ref[h, ci, c:2 * c, :] = (q * egc).astype(BF16)
                kdt_ref[h, ci] = (k * jnp.exp(g_last - gc)).T.astype(BF16)
                egl_ref[h, ci] = jnp.broadcast_to(egc[c - bb:c, :], (bb, dk))
                v = conv(2 * n_kh + h, r0)
                rhs = jnp.concatenate([v * beta, k * (beta * egc)], axis=1)
                probs.append((h, ci, jnp.where(low, kk * beta * decay, 0.0), rhs))
        t_mats = _unit_lower_inverse_each([p[2] for p in probs], c)
        uws = [_dot(t, p[3]) for t, p in zip(t_mats, probs)]
        for (h, ci, _, _), uw in zip(probs, uws):
            u_ref[h, ci * c:(ci + 1) * c, :] = uw[:, :dk]
            wq_ref[h, ci, 0:c, :] = uw[:, dk:].astype(BF16)

    kh_per_iter = max(1, INTRA_UNITS // n_chunks)

    def intra_body(it, carry):
        intra([(it * kh_per_iter + m, ci) for m in range(kh_per_iter) for ci in range(n_chunks)])
        return carry

    lax.fori_loop(0, n_kh // kh_per_iter, intra_body, 0)

    def inter(ci, heads):
        rows = pl.ds(pl.multiple_of(ci * c, c), c)
        mm = lambda x, y: jnp.dot(x, y, preferred_element_type=F32)
        res = [[mm(wq_ref[h, ci], sn_ref[s, h].astype(BF16)) for s in range(bb)] for h in heads]
        v_news, o_states = [], []
        for h, res_h in zip(heads, res):
            v_new = u_ref[h, rows, :]
            o_state = None
            for s, r in enumerate(res_h):
                if bb == 1:
                    v_new = v_new - r[:c]
                    o_state = r[c:]
                else:
                    mine = member == s
                    v_new = v_new - jnp.where(mine, r[:c], 0.0)
                    o_state = jnp.where(mine, r[c:], 0.0 if o_state is None else o_state)
            v_news.append(v_new)
            o_states.append(o_state)
        v_seq = [[(v if bb == 1 else jnp.where(member == s, v, 0.0)).astype(BF16) for s in range(bb)]
                 for v in v_news]
        attn_v = [mm(attn_ref[h, rows, :], v.astype(BF16)) for h, v in zip(heads, v_news)]
        upd = [[mm(kdt_ref[h, ci], v_s) for v_s in v_h] for h, v_h in zip(heads, v_seq)]
        for h, o_state, av, upd_h in zip(heads, o_states, attn_v, upd):
            for s in range(bb):
                sn_ref[s, h] = sn_ref[s, h] * egl_ref[h, ci, s:s + 1, :] + upd_h[s]
            z = hm_ref[n_conv + h, rows, :]
            o_ref[h, rows, :] = _rms(o_state + av, nw_ref[...]) * _silu(z)

    def inter_body(ci, carry):
        if bb == 1:
            inter(ci, list(range(hv)))
        else:
            def head_body(h, carry2):
                inter(ci, [h])
                return carry2
            lax.fori_loop(0, hv, head_body, 0)
        return carry

    lax.fori_loop(0, n_chunks, inter_body, 0)


def _dn_seq(hm, beta, g, conv_past, conv_w, s0, norm_w, c, bb, rows_per_step):
    n_blk, t, dk = hm.shape
    n_grp = s0.shape[0] // bb
    hv = s0.shape[1]
    n_kh = hv // 2
    n_conv = 4 * n_kh
    rows_per_grp = t // n_grp
    steps = rows_per_grp // rows_per_step
    n_chunks = rows_per_step // c
    halo = CONV_PAST * bb
    off = -(-halo // 8) * 8
    assert n_blk == n_conv + hv and rows_per_step % c == 0 and rows_per_grp % rows_per_step == 0
    assert bb & (bb - 1) == 0 and c & (c - 1) == 0 and c % bb == 0
    kern = functools.partial(_dn_seq_kernel, c=c, bb=bb, n_kh=n_kh)
    tok = lambda i, j: i * steps + j
    state = pl.BlockSpec((bb, hv, dk, dk), lambda i, j: (i, 0, 0, 0))
    gate = pl.BlockSpec((rows_per_step, hv), lambda i, j: (tok(i, j), 0))
    return pl.pallas_call(
        kern,
        grid=(n_grp, steps),
        in_specs=[pl.BlockSpec((n_blk, rows_per_step, dk), lambda i, j: (0, tok(i, j), 0)),
                  gate, gate,
                  pl.BlockSpec((1, n_conv, halo, dk), lambda i, j: (i, 0, 0, 0)),
                  _resident(conv_w.shape), state, _resident(norm_w.shape)],
        out_specs=[pl.BlockSpec((hv, rows_per_step, dk), lambda i, j: (0, tok(i, j), 0)), state],
        out_shape=[jax.ShapeDtypeStruct((hv, t, dk), F32), jax.ShapeDtypeStruct(s0.shape, F32)],
        scratch_shapes=[pltpu.VMEM((n_conv, off + rows_per_step, dk), F32),
                        pltpu.VMEM((hv, rows_per_step, dk), F32),
                        pltpu.VMEM((hv, n_chunks, 2 * c, dk), BF16),
                        pltpu.VMEM((hv, rows_per_step, c), BF16),
                        pltpu.VMEM((hv, n_chunks, dk, c), BF16),
                        pltpu.VMEM((hv, n_chunks, bb, dk), F32)],
        compiler_params=_params("parallel", "arbitrary"),
        name="dn_seq",
    )(hm, beta, g, conv_past, conv_w, s0, norm_w)


def _dn_out_kernel(o_ref, x_ref, ng_ref, w_ref, y_ref, *, post):
    o = jnp.concatenate([o_ref[h] for h in range(o_ref.shape[0])], axis=1)
    y = _dot(o, w_ref[...])
    y_ref[...] = x_ref[...] + _rms(y, ng_ref[post:post + 1, :])


def _dn_out(o, x, ng, w_out, post):
    t, d = x.shape
    hv, _, dk = o.shape
    tm = min(TOKEN_TILE, t)
    kern = functools.partial(_dn_out_kernel, post=post)
    return pl.pallas_call(
        kern,
        grid=(t // tm,),
        in_specs=[pl.BlockSpec((hv, tm, dk), lambda i: (0, i, 0)),
                  pl.BlockSpec((tm, d), lambda i: (i, 0)),
                  _resident(ng.shape), _resident(w_out.shape)],
        out_specs=pl.BlockSpec((tm, d), lambda i: (i, 0)),
        out_shape=jax.ShapeDtypeStruct((t, d), F32),
        compiler_params=_params("parallel"),
        name="dn_out",
    )(o, x, ng, w_out)


def _dn_layer(x, conv_past, s0, ng, w_main, w_b, w_a, a_log, dt_bias, conv_w, norm_w, w_out):
    b, s, d = x.shape
    dk = HEAD_DIM
    conv_dim = conv_past.shape[-1]
    n_conv = conv_dim // dk
    assert s >= CONV_PAST
    if s >= CHUNK:
        bb, c, rows_per_step = 1, CHUNK, min(s, 4 * CHUNK)
    else:
        bb = min(SHORT_GROUP, b)
        c = rows_per_step = s * bb
    n_grp = b // bb
    xt = x.reshape(n_grp, bb, s, d).transpose(0, 2, 1, 3).reshape(b * s, d)
    hm, beta, g = _dn_proj(xt, ng, w_main, w_b, w_a, a_log, dt_bias, pre=2)
    cp = conv_past.reshape(n_grp, bb, CONV_PAST, n_conv, dk).transpose(0, 3, 2, 1, 4)
    cp = cp.reshape(n_grp, n_conv, CONV_PAST * bb, dk)
    cw = conv_w.reshape(CONV_W, n_conv, dk).transpose(1, 0, 2)
    o, s_new = _dn_seq(hm, beta, g, cp, cw, s0, norm_w, c, bb, rows_per_step)
    y = _dn_out(o, xt, ng, w_out, post=3)
    y = y.reshape(n_grp, s, bb, d).transpose(0, 2, 1, 3).reshape(b, s, d)
    tail = hm[:n_conv].reshape(n_conv, n_grp, s, bb, dk)[:, :, s - CONV_PAST:]
    new_conv = tail.transpose(1, 3, 2, 0, 4).reshape(b, CONV_PAST, conv_dim)
    return y, new_conv, s_new


def _trunk(x, st_pool, st_conv, st_delta, pos0, p):
    b, s, d = x.shape
    depth = p["wg"].shape[0]
    new_pool, new_conv, new_delta = [], [], []
    for i in range(depth):
        ng = p["ng"][i]
        x = _ffn(x.reshape(b * s, d), ng, p["wg"][i, 0], p["wu"][i, 0], p["wd"][i, 0], 0, 1).reshape(b, s, d)
        j = i // 2
        if i % 2 == 0:
            if s >= 2 * POOL_PAST:
                x, ps = _pool_seq(x, st_pool[j], ng, p["pool_w"][j], p["pool_scale"][j], pos0, 2, 3)
            else:
                xt, pt = _pool_step(x.transpose(1, 0, 2), st_pool[j].transpose(1, 0, 2), ng,
                                    p["pool_w"][j], p["pool_scale"][j], pos0, 2, 3)
                x, ps = xt.transpose(1, 0, 2), pt.transpose(1, 0, 2)
            new_pool.append(ps)
        else:
            x, cs, ds = _dn_layer(x, st_conv[j], st_delta[j], ng, p["w_main"][j], p["w_b"][j],
                                  p["w_a"][j], p["a_log"][j], p["dt_bias"][j], p["conv_w"][j],
                                  p["norm_w"][j], p["w_out"][j])
            new_conv.append(cs)
            new_delta.append(ds)
        x = _ffn(x.reshape(b * s, d), ng, p["wg"][i, 1], p["wu"][i, 1], p["wd"][i, 1], 4, 5).reshape(b, s, d)
    return x, jnp.stack(new_pool), jnp.stack(new_conv), jnp.stack(new_delta)


def kernel(x_prompt, x_sample, state_pool, state_conv, state_delta, norm_gains, w_ffn_gate,
           w_ffn_up, w_ffn_down, pool_w, pool_scale, dn_w_in, dn_conv_w, dn_a_log, dn_dt_bias,
           dn_norm_w, dn_w_out):
    n_dn, _, _, conv_dim = state_conv.shape
    hv = state_delta.shape[2]
    val_dim = hv * HEAD_DIM
    n_pool = state_pool.shape[0]
    d = x_prompt.shape[-1]
    p = {
        "ng": norm_gains,
        "wg": w_ffn_gate.astype(BF16), "wu": w_ffn_up.astype(BF16), "wd": w_ffn_down.astype(BF16),
        "pool_w": pool_w.astype(BF16), "pool_scale": pool_scale.reshape(n_pool, 1, d),
        "w_main": dn_w_in[:, :, :conv_dim + val_dim].astype(BF16),
        "w_b": dn_w_in[:, :, conv_dim + val_dim:conv_dim + val_dim + hv].astype(BF16),
        "w_a": dn_w_in[:, :, conv_dim + val_dim + hv:].astype(BF16),
        "a_log": dn_a_log.reshape(n_dn, 1, hv), "dt_bias": dn_dt_bias.reshape(n_dn, 1, hv),
        "conv_w": dn_conv_w, "norm_w": dn_norm_w.reshape(n_dn, 1, HEAD_DIM),
        "w_out": dn_w_out.astype(BF16),
    }
    bp = x_prompt.shape[0]
    zp = jnp.zeros((n_pool, bp) + state_pool.shape[2:], F32)
    zc = jnp.zeros((n_dn, bp) + state_conv.shape[2:], F32)
    zd = jnp.zeros((n_dn, bp) + state_delta.shape[2:], F32)
    y_p, pool_p, conv_p, delta_p = _trunk(x_prompt, zp, zc, zd, 0, p)
    y_s, pool_s, conv_s, delta_s = _trunk(x_sample, state_pool, state_conv, state_delta, PAST_LEN, p)
    return (y_p, y_s, pool_p, conv_p, delta_p, pool_s, conv_s, delta_s)
```

```python
import functools
from typing import NamedTuple

import jax
import jax.numpy as jnp
from jax import lax
from jax.experimental import pallas as pl
from jax.experimental.pallas import tpu as pltpu

F32 = jnp.float32
BF16 = jnp.bfloat16

EPS = 1e-6
PAST_LEN = 16384
POOL_WINDOWS = (2, 4, 8, 16)
POOL_PAST = max(POOL_WINDOWS) - 1
HEAD_DIM = 128
CONV_W = 4
CONV_PAST = CONV_W - 1
CHUNK = 64
SHORT_GROUP = 8
INV_BASE = 16
INTRA_UNITS = 16

VMEM_LIMIT = 56 * 1024 * 1024
TOKEN_TILE = 512


def _params(*sem):
    return pltpu.CompilerParams(dimension_semantics=sem, vmem_limit_bytes=VMEM_LIMIT)


class _Res(NamedTuple):
    arr: jax.Array
    lead: tuple = ()

    @property
    def spec(self):
        n = len(self.lead)
        index = tuple(self.lead) + (0,) * (self.arr.ndim - n)
        return pl.BlockSpec((None,) * n + tuple(self.arr.shape[n:]), lambda *_: index,
                            pipeline_mode=pl.Buffered(1))


def _rms(x, g):
    return x * lax.rsqrt(jnp.mean(x * x, axis=-1, keepdims=True) + EPS) * g


def _silu(x):
    return x / (1.0 + jnp.exp(-x))


def _dot(a, b):
    return jnp.dot(a.astype(BF16), b.astype(BF16), preferred_element_type=F32)


def _dot_nt(a, b):
    return lax.dot_general(a.astype(BF16), b.astype(BF16), (((1,), (1,)), ((), ())),
                           preferred_element_type=F32)


def _ffn_kernel(x_ref, ng_ref, wg_ref, wu_ref, wd_ref, o_ref, *, pre, post, tf):
    x = x_ref[...]
    h = _rms(x, ng_ref[pre:pre + 1, :]).astype(BF16)
    acc = jnp.zeros(x.shape, F32)
    for c in range(wg_ref.shape[1] // tf):
        sl = slice(c * tf, (c + 1) * tf)
        gate = jnp.dot(h, wg_ref[:, sl], preferred_element_type=F32)
        up = jnp.dot(h, wu_ref[:, sl], preferred_element_type=F32)
        acc = acc + _dot(_silu(gate) * up, wd_ref[sl, :])
    o_ref[...] = x + 0.5 * _rms(acc, ng_ref[post:post + 1, :])


def _ffn(x, ng, wg, wu, wd, pre, post):
    t, d = x.shape
    tm = min(TOKEN_TILE, t)
    kern = functools.partial(_ffn_kernel, pre=pre, post=post, tf=256)
    return pl.pallas_call(
        kern,
        grid=(t // tm,),
        in_specs=[pl.BlockSpec((tm, d), lambda i: (i, 0)), ng.spec, wg.spec, wu.spec, wd.spec],
        out_specs=pl.BlockSpec((tm, d), lambda i: (i, 0)),
        out_shape=jax.ShapeDtypeStruct((t, d), F32),
        compiler_params=_params("parallel"),
        name="ffn",
    )(x, ng.arr, wg.arr, wu.arr, wd.arr)


def _pool_seq_kernel(x_ref, past_ref, ng_ref, w_ref, sc_ref, o_ref, np_ref, hp_ref,
                     *, ts, pos0, pre, post):
    s = pl.program_id(1)
    off = POOL_PAST + 1
    d = x_ref.shape[-1]
    gw = d // len(POOL_WINDOWS)

    @pl.when(s == 0)
    def _():
        hp_ref[0:1, :] = jnp.zeros((1, d), F32)
        hp_ref[1:off, :] = past_ref[0]

    x = x_ref[0]
    h = _rms(x, ng_ref[pre:pre + 1, :])
    hp_ref[off:off + ts, :] = h
    pos = pos0 + s * ts + lax.broadcasted_iota(jnp.int32, (ts, 1), 0)
    outs = []
    for gi, w in enumerate(POOL_WINDOWS):
        lo, hi = gi * gw, (gi + 1) * gw
        hg = h[:, lo:hi]
        win = hg
        for k in range(1, w):
            win = win + hp_ref[off - k:off - k + ts, lo:hi]
        cnt = jnp.minimum(pos + 1, w).astype(F32)
        outs.append(_dot(win / cnt - hg, w_ref[gi]))
    y = jnp.concatenate(outs, axis=-1) * sc_ref[...]
    o_ref[0] = x + _rms(y, ng_ref[post:post + 1, :])
    tail = hp_ref[ts + 1:ts + off, :]
    np_ref[0] = tail
    hp_ref[1:off, :] = tail


def _pool_seq(x, past, ng, w, sc, pos0, pre, post):
    b, s, d = x.shape
    ts = min(TOKEN_TILE, s)
    kern = functools.partial(_pool_seq_kernel, ts=ts, pos0=pos0, pre=pre, post=post)
    return pl.pallas_call(
        kern,
        grid=(b, s // ts),
        in_specs=[pl.BlockSpec((1, ts, d), lambda i, j: (i, j, 0)),
                  pl.BlockSpec((1, POOL_PAST, d), lambda i, j: (i, 0, 0)),
                  ng.spec, w.spec, sc.spec],
        out_specs=[pl.BlockSpec((1, ts, d), lambda i, j: (i, j, 0)),
                   pl.BlockSpec((1, POOL_PAST, d), lambda i, j: (i, 0, 0))],
        out_shape=[jax.ShapeDtypeStruct((b, s, d), F32),
                   jax.ShapeDtypeStruct((b, POOL_PAST, d), F32)],
        scratch_shapes=[pltpu.VMEM((POOL_PAST + 1 + ts, d), F32)],
        compiler_params=_params("parallel", "arbitrary"),
        name="pool_seq",
    )(x, past, ng.arr, w.arr, sc.arr)


def _pool_step_kernel(x_ref, past_ref, ng_ref, w_ref, sc_ref, o_ref, np_ref, *, pos0, pre, post):
    steps, bb, d = x_ref.shape
    gw = d // len(POOL_WINDOWS)
    xs = [x_ref[t] for t in range(steps)]
    hs = [_rms(x, ng_ref[pre:pre + 1, :]) for x in xs]
    hp = [past_ref[p] for p in range(POOL_PAST)] + hs
    outs = []
    for gi, w in enumerate(POOL_WINDOWS):
        lo, hi = gi * gw, (gi + 1) * gw
        rows = []
        for t in range(steps):
            win = hp[POOL_PAST + t][:, lo:hi]
            for k in range(1, w):
                win = win + hp[POOL_PAST + t - k][:, lo:hi]
            cnt = float(min(pos0 + t + 1, w))
            rows.append(win / cnt - hs[t][:, lo:hi])
        outs.append(_dot(jnp.concatenate(rows, axis=0), w_ref[gi]))
    y = jnp.concatenate(outs, axis=-1) * sc_ref[...]
    for t in range(steps):
        o_ref[t] = xs[t] + _rms(y[t * bb:(t + 1) * bb], ng_ref[post:post + 1, :])
    for p in range(POOL_PAST):
        np_ref[p] = hp[steps + p]


def _pool_step(x, past, ng, w, sc, pos0, pre, post):
    steps, b, d = x.shape
    bb = min(32, b)
    kern = functools.partial(_pool_step_kernel, pos0=pos0, pre=pre, post=post)
    return pl.pallas_call(
        kern,
        grid=(b // bb,),
        in_specs=[pl.BlockSpec((steps, bb, d), lambda i: (0, i, 0)),
                  pl.BlockSpec((POOL_PAST, bb, d), lambda i: (0, i, 0)),
                  ng.spec, w.spec, sc.spec],
        out_specs=[pl.BlockSpec((steps, bb, d), lambda i: (0, i, 0)),
                   pl.BlockSpec((POOL_PAST, bb, d), lambda i: (0, i, 0))],
        out_shape=[jax.ShapeDtypeStruct((steps, b, d), F32),
                   jax.ShapeDtypeStruct((POOL_PAST, b, d), F32)],
        compiler_params=_params("parallel"),
        name="pool_step",
    )(x, past, ng.arr, w.arr, sc.arr)


def _dn_proj_kernel(x_ref, ng_ref, w_ref, alog_ref, dtb_ref, hm_ref, beta_ref, g_ref, *, pre, tn):
    h = _rms(x_ref[...], ng_ref[pre:pre + 1, :]).astype(BF16)
    n_blk = hm_ref.shape[0]
    hv = beta_ref.shape[1]
    per = tn // HEAD_DIM
    for c in range(n_blk // per):
        res = jnp.dot(h, w_ref[:, c * tn:(c + 1) * tn], preferred_element_type=F32)
        for m in range(per):
            hm_ref[c * per + m] = res[:, m * HEAD_DIM:(m + 1) * HEAD_DIM]
    gates = jnp.dot(h, w_ref[:, n_blk * HEAD_DIM:], preferred_element_type=F32)
    b = gates[:, :hv]
    a = gates[:, hv:] + dtb_ref[...]
    beta_ref[...] = 1.0 / (1.0 + jnp.exp(-b))
    softplus = jnp.maximum(a, 0.0) + jnp.log1p(jnp.exp(-jnp.abs(a)))
    g_ref[...] = -jnp.exp(alog_ref[...]) * softplus


def _dn_proj(x, ng, w_in, a_log, dt_bias, pre):
    t, d = x.shape
    tm = min(256, t)
    hv = a_log.arr.shape[-1]
    n_blk = (w_in.arr.shape[-1] - 2 * hv) // HEAD_DIM
    kern = functools.partial(_dn_proj_kernel, pre=pre, tn=512)
    row = lambda n: pl.BlockSpec((tm, n), lambda i: (i, 0))
    return pl.pallas_call(
        kern,
        grid=(t // tm,),
        in_specs=[row(d), ng.spec, w_in.spec, a_log.spec, dt_bias.spec],
        out_specs=[pl.BlockSpec((n_blk, tm, HEAD_DIM), lambda i: (0, i, 0)), row(hv), row(hv)],
        out_shape=[jax.ShapeDtypeStruct((n_blk, t, HEAD_DIM), F32),
                   jax.ShapeDtypeStruct((t, hv), F32), jax.ShapeDtypeStruct((t, hv), F32)],
        compiler_params=_params("parallel"),
        name="dn_proj",
    )(x, ng.arr, w_in.arr, a_log.arr, dt_bias.arr)


def _dot3_each(pairs):
    split = lambda x: (x.astype(BF16), (x - x.astype(BF16).astype(F32)).astype(BF16))
    parts = [(split(a), split(b)) for a, b in pairs]
    mm = lambda x, y: jnp.dot(x, y, preferred_element_type=F32)
    hh = [mm(a[0], b[0]) for a, b in parts]
    hl = [mm(a[0], b[1]) for a, b in parts]
    lh = [mm(a[1], b[0]) for a, b in parts]
    return [x + (y + z) for x, y, z in zip(hh, hl, lh)]


def _unit_lower_inverse_each(l_mats, c):
    row = lax.broadcasted_iota(jnp.int32, (c, c), 0)
    col = lax.broadcasted_iota(jnp.int32, (c, c), 1)
    blk = jnp.bitwise_xor(row, col)
    eye = jnp.where(row == col, 1.0, 0.0)
    bs = min(INV_BASE, c)
    ps = [jnp.where(blk < bs, -l, 0.0) for l in l_mats]
    ts = [eye + p for p in ps]
    n = 2
    while n < bs:
        ps = [_dot(p, p) for p in ps]
        tps = [_dot(t, p) for t, p in zip(ts, ps)]
        ts = [t + tp for t, tp in zip(ts, tps)]
        n *= 2
    while bs < c:
        es = [jnp.where((blk < 2 * bs) & (blk >= bs), l, 0.0) for l in l_mats]
        tes = [_dot(t, e) for t, e in zip(ts, es)]
        tets = [_dot(te, t) for te, t in zip(tes, ts)]
        ts = [t - tet for t, tet in zip(ts, tets)]
        bs *= 2
    lts = _dot3_each(list(zip(l_mats, ts)))
    trs = [_dot(t, eye - t - lt) for t, lt in zip(ts, lts)]
    return [t + tr for t, tr in zip(ts, trs)]


def _dn_seq_kernel(stack_ref, hm_ref, beta_ref, g_ref, cp_ref, cw_ref, s0_ref, nw_ref,
                   o_ref, sn_ref,
                   raw_ref, u_ref, wq_ref, attn_ref, kdt_ref, egl_ref, *, c, bb, n_kh):
    step = pl.program_id(1)
    dk = HEAD_DIM
    hv = 2 * n_kh
    n_conv = 4 * n_kh
    rows_per_step = hm_ref.shape[1]
    n_chunks = rows_per_step // c
    halo = CONV_PAST * bb
    off = raw_ref.shape[1] - rows_per_step

    @pl.when(step == 0)
    def _():
        raw_ref[:, off - halo:off, :] = cp_ref[0]
        sn_ref[...] = s0_ref[...]

    @pl.when(step > 0)
    def _():
        raw_ref[:, off - halo:off, :] = raw_ref[:, off + rows_per_step - halo:off + rows_per_step, :]

    raw_ref[:, off:off + rows_per_step, :] = hm_ref[0:n_conv]

    row = lax.broadcasted_iota(jnp.int32, (c, c), 0)
    col = lax.broadcasted_iota(jnp.int32, (c, c), 1)
    same = (jnp.bitwise_xor(row, col) & (bb - 1)) == 0
    low_eq = same & (row >= col)
    low = same & (row > col)
    up_eq = same & (row <= col)
    last = same & (col >= c - bb)
    eye = row == col
    head_lane = lax.broadcasted_iota(jnp.int32, (1, hv), 1)
    member = lax.broadcasted_iota(jnp.int32, (c, 1), 0) & (bb - 1)

    def conv(cb, r0):
        acc = None
        for j in range(CONV_W):
            lo = off + r0 - (CONV_PAST - j) * bb
            term = raw_ref[cb, lo:lo + c, :] * cw_ref[cb, j:j + 1, :]
            acc = term if acc is None else acc + term
        return _silu(acc)


    def intra(units):
        qs, ks = [], []
        for hk, ci in units:
            qc = conv(hk, ci * c)
            kc = conv(n_kh + hk, ci * c)
            qs.append(qc * lax.rsqrt(jnp.sum(qc * qc, axis=-1, keepdims=True) + EPS) * (dk ** -0.5))
            ks.append(kc * lax.rsqrt(jnp.sum(kc * kc, axis=-1, keepdims=True) + EPS))
        kks = [_dot_nt(k, k) for k in ks]
        qks = [_dot_nt(q, k) for q, k in zip(qs, ks)]
        probs = []
        for (hk, ci), q, k, kk, qk in zip(units, qs, ks, kks, qks):
            r0 = ci * c
            beta_blk = beta_ref[r0:r0 + c, :]
            g_blk = g_ref[r0:r0 + c, :]
            for j in range(2):
                h = 2 * hk + j
                sel = head_lane == h
                beta = jnp.sum(jnp.where(sel, beta_blk, 0.0), axis=1, keepdims=True)
                g = jnp.sum(jnp.where(sel, g_blk, 0.0), axis=1, keepdims=True)
                g_row = jnp.sum(jnp.where(eye, g, 0.0), axis=0, keepdims=True)
                gc = jnp.sum(jnp.where(low_eq, g_row, 0.0), axis=1, keepdims=True)
                gc_row = jnp.sum(jnp.where(up_eq, g, 0.0), axis=0, keepdims=True)
                g_last = jnp.sum(jnp.where(last, gc_row, 0.0), axis=1, keepdims=True)
                decay = jnp.exp(jnp.where(low_eq, gc - gc_row, -jnp.inf))
                egc = jnp.exp(gc)
                attn_ref[h, r0:r0 + c, :] = (qk * decay).astype(BF16)
                wq_ref[h, ci, c:2 * c, :] = (q * egc).astype(BF16)
                kdt_ref[h, ci] = (k * jnp.exp(g_last - gc)).T.astype(BF16)
                egl_ref[h, ci] = jnp.broadcast_to(egc[c - bb:c, :], (bb, dk))
                v = conv(2 * n_kh + h, r0)
                rhs = jnp.concatenate([v * beta, k * (beta * egc)], axis=1)
                probs.append((h, ci, jnp.where(low, kk * beta * decay, 0.0), rhs))
        t_mats = _unit_lower_inverse_each([p[2] for p in probs], c)
        uws = [_dot(t, p[3]) for t, p in zip(t_mats, probs)]
        for (h, ci, _, _), uw in zip(probs, uws):
            u_ref[h, ci * c:(ci + 1) * c, :] = uw[:, :dk]
            wq_ref[h, ci, 0:c, :] = uw[:, dk:].astype(BF16)

    kh_per_iter = min(n_kh, max(1, INTRA_UNITS // n_chunks))

    def intra_body(it, carry):
        intra([(it * kh_per_iter + m, ci) for m in range(kh_per_iter) for ci in range(n_chunks)])
        return carry

    lax.fori_loop(0, n_kh // kh_per_iter, intra_body, 0)

    def inter(ci, heads):
        rows = pl.ds(pl.multiple_of(ci * c, c), c)
        mm = lambda x, y: jnp.dot(x, y, preferred_element_type=F32)
        res = [[mm(wq_ref[h, ci], sn_ref[s, h].astype(BF16)) for s in range(bb)] for h in heads]
        v_news, o_states = [], []
        for h, res_h in zip(heads, res):
            v_new = u_ref[h, rows, :]
            o_state = None
            for s, r in enumerate(res_h):
                if bb == 1:
                    v_new = v_new - r[:c]
                    o_state = r[c:]
                else:
                    mine = member == s
                    v_new = v_new - jnp.where(mine, r[:c], 0.0)
                    o_state = jnp.where(mine, r[c:], 0.0 if o_state is None else o_state)
            v_news.append(v_new)
            o_states.append(o_state)
        v_seq = [[(v if bb == 1 else jnp.where(member == s, v, 0.0)).astype(BF16) for s in range(bb)]
                 for v in v_news]
        attn_v = [mm(attn_ref[h, rows, :], v.astype(BF16)) for h, v in zip(heads, v_news)]
        upd = [[mm(kdt_ref[h, ci], v_s) for v_s in v_h] for h, v_h in zip(heads, v_seq)]
        for h, o_state, av, upd_h in zip(heads, o_states, attn_v, upd):
            for s in range(bb):
                sn_ref[s, h] = sn_ref[s, h] * egl_ref[h, ci, s:s + 1, :] + upd_h[s]
            z = hm_ref[n_conv + h, rows, :]
            o_ref[h, rows, :] = _rms(o_state + av, nw_ref[...]) * _silu(z)

    def inter_body(ci, carry):
        if bb == 1:
            inter(ci, list(range(hv)))
        else:
            def head_body(h, carry2):
                inter(ci, [h])
                return carry2
            lax.fori_loop(0, hv, head_body, 0)
        return carry

    lax.fori_loop(0, n_chunks, inter_body, 0)


def _dn_seq(hm, beta, g, conv_past, conv_w, s0_all, layer, stack, norm_w, c, bb, rows_per_step):
    n_blk, t, dk = hm.shape
    n_grp = s0_all.shape[1] // bb
    hv = s0_all.shape[2]
    n_kh = hv // 2
    n_conv = 4 * n_kh
    rows_per_grp = t // n_grp
    steps = rows_per_grp // rows_per_step
    n_chunks = rows_per_step // c
    halo = CONV_PAST * bb
    off = -(-halo // 8) * 8
    assert n_blk == n_conv + hv and rows_per_step % c == 0 and rows_per_grp % rows_per_step == 0
    assert bb & (bb - 1) == 0 and c & (c - 1) == 0 and c % bb == 0
    kern = functools.partial(_dn_seq_kernel, c=c, bb=bb, n_kh=n_kh)
    tok = lambda i, j: i * steps + j
    state = pl.BlockSpec((None, bb, hv, dk, dk), lambda i, j: (layer, i, 0, 0, 0))
    gate = pl.BlockSpec((rows_per_step, hv), lambda i, j: (tok(i, j), 0))
    return pl.pallas_call(
        kern,
        grid=(n_grp, steps),
        in_specs=[pl.BlockSpec(memory_space=pl.ANY),
                  pl.BlockSpec((n_blk, rows_per_step, dk), lambda i, j: (0, tok(i, j), 0)),
                  gate, gate,
                  pl.BlockSpec((1, n_conv, halo, dk), lambda i, j: (i, 0, 0, 0)),
                  conv_w.spec, state, norm_w.spec],
        out_specs=[pl.BlockSpec((hv, rows_per_step, dk), lambda i, j: (0, tok(i, j), 0)), state],
        out_shape=[jax.ShapeDtypeStruct((hv, t, dk), F32), jax.ShapeDtypeStruct(s0_all.shape, F32)],
        scratch_shapes=[pltpu.VMEM((n_conv, off + rows_per_step, dk), F32),
                        pltpu.VMEM((hv, rows_per_step, dk), F32),
                        pltpu.VMEM((hv, n_chunks, 2 * c, dk), BF16),
                        pltpu.VMEM((hv, rows_per_step, c), BF16),
                        pltpu.VMEM((hv, n_chunks, dk, c), BF16),
                        pltpu.VMEM((hv, n_chunks, bb, dk), F32)],
        input_output_aliases={} if stack is None else {0: 1},
        compiler_params=_params("parallel", "arbitrary"),
        name="dn_seq",
    )(s0_all if stack is None else stack, hm, beta, g, conv_past, conv_w.arr, s0_all, norm_w.arr)


def _dn_out_kernel(o_ref, x_ref, ng_ref, w_ref, y_ref, *, post):
    o = jnp.concatenate([o_ref[h] for h in range(o_ref.shape[0])], axis=1)
    y = _dot(o, w_ref[...])
    y_ref[...] = x_ref[...] + _rms(y, ng_ref[post:post + 1, :])


def _dn_out(o, x, ng, w_out, post):
    t, d = x.shape
    hv, _, dk = o.shape
    tm = min(TOKEN_TILE, t)
    kern = functools.partial(_dn_out_kernel, post=post)
    return pl.pallas_call(
        kern,
        grid=(t // tm,),
        in_specs=[pl.BlockSpec((hv, tm, dk), lambda i: (0, i, 0)),
                  pl.BlockSpec((tm, d), lambda i: (i, 0)), ng.spec, w_out.spec],
        out_specs=pl.BlockSpec((tm, d), lambda i: (i, 0)),
        out_shape=jax.ShapeDtypeStruct((t, d), F32),
        compiler_params=_params("parallel"),
        name="dn_out",
    )(o, x, ng.arr, w_out.arr)


def _dn_layer(x, conv_past, s0_all, layer, stack, ng, w_in, a_log, dt_bias, conv_w, norm_w, w_out):
    b, s, d = x.shape
    dk = HEAD_DIM
    conv_dim = conv_past.shape[-1]
    n_conv = conv_dim // dk
    assert s >= CONV_PAST
    if s >= CHUNK:
        bb, c, rows_per_step = 1, CHUNK, min(s, 4 * CHUNK)
    else:
        bb = min(SHORT_GROUP, b)
        c = rows_per_step = s * bb
    n_grp = b // bb
    xt = x.reshape(n_grp, bb, s, d).transpose(0, 2, 1, 3).reshape(b * s, d)
    hm, beta, g = _dn_proj(xt, ng, w_in, a_log, dt_bias, pre=2)
    cp = conv_past.reshape(n_grp, bb, CONV_PAST, n_conv, dk).transpose(0, 3, 2, 1, 4)
    cp = cp.reshape(n_grp, n_conv, CONV_PAST * bb, dk)
    o, stack = _dn_seq(hm, beta, g, cp, conv_w, s0_all, layer, stack, norm_w, c, bb, rows_per_step)
    y = _dn_out(o, xt, ng, w_out, post=3)
    y = y.reshape(n_grp, s, bb, d).transpose(0, 2, 1, 3).reshape(b, s, d)
    tail = hm.reshape(-1, n_grp, s, bb, dk)[:n_conv, :, s - CONV_PAST:]
    new_conv = tail.transpose(1, 3, 2, 0, 4).reshape(b, CONV_PAST, conv_dim)
    return y, new_conv, stack


def _trunk(x, st_pool, st_conv, st_delta, pos0, p):
    b, s, d = x.shape
    depth = p["wg"].shape[0]
    new_pool, new_conv, new_delta = [], [], None
    for i in range(depth):
        ng = _Res(p["ng"], (i,))
        ffn_w = lambda half: [_Res(p[k], (i, half)) for k in ("wg", "wu", "wd")]
        x = _ffn(x.reshape(b * s, d), ng, *ffn_w(0), 0, 1).reshape(b, s, d)
        j = i // 2
        if i % 2 == 0:
            pool_w, pool_sc = _Res(p["pool_w"], (j,)), _Res(p["pool_scale"], (j,))
            if s >= 2 * POOL_PAST:
                x, ps = _pool_seq(x, st_pool[j], ng, pool_w, pool_sc, pos0, 2, 3)
            else:
                xt, pt = _pool_step(x.transpose(1, 0, 2), st_pool[j].transpose(1, 0, 2), ng,
                                    pool_w, pool_sc, pos0, 2, 3)
                x, ps = xt.transpose(1, 0, 2), pt.transpose(1, 0, 2)
            new_pool.append(ps)
        else:
            x, cs, new_delta = _dn_layer(
                x, st_conv[j], st_delta, j, new_delta, ng, _Res(p["w_in"], (j,)),
                _Res(p["a_log"], (j,)), _Res(p["dt_bias"], (j,)), _Res(p["conv_w"], (j,)),
                _Res(p["norm_w"], (j,)), _Res(p["w_out"], (j,)))
            new_conv.append(cs)
        x = _ffn(x.reshape(b * s, d), ng, *ffn_w(1), 4, 5).reshape(b, s, d)
    return x, jnp.stack(new_pool), jnp.stack(new_conv), new_delta


def kernel(x_prompt, x_sample, state_pool, state_conv, state_delta, norm_gains, w_ffn_gate,
           w_ffn_up, w_ffn_down, pool_w, pool_scale, dn_w_in, dn_conv_w, dn_a_log, dn_dt_bias,
           dn_norm_w, dn_w_out):
    n_dn, _, _, conv_dim = state_conv.shape
    hv = state_delta.shape[2]
    n_pool = state_pool.shape[0]
    d = x_prompt.shape[-1]
    n_conv = conv_dim // HEAD_DIM
    p = {
        "ng": norm_gains,
        "wg": w_ffn_gate.astype(BF16), "wu": w_ffn_up.astype(BF16), "wd": w_ffn_down.astype(BF16),
        "pool_w": pool_w.astype(BF16), "pool_scale": pool_scale.reshape(n_pool, 1, d),
        "w_in": dn_w_in.astype(BF16),
        "a_log": dn_a_log.reshape(n_dn, 1, hv), "dt_bias": dn_dt_bias.reshape(n_dn, 1, hv),
        "conv_w": dn_conv_w.reshape(n_dn, CONV_W, n_conv, HEAD_DIM).transpose(0, 2, 1, 3),
        "norm_w": dn_norm_w.reshape(n_dn, 1, HEAD_DIM),
        "w_out": dn_w_out.astype(BF16),
    }
    bp = x_prompt.shape[0]
    zp = jnp.zeros((n_pool, bp) + state_pool.shape[2:], F32)
    zc = jnp.zeros((n_dn, bp) + state_conv.shape[2:], F32)
    zd = jnp.zeros((n_dn, bp) + state_delta.shape[2:], F32)
    y_p, pool_p, conv_p, delta_p = _trunk(x_prompt, zp, zc, zd, 0, p)
    y_s, pool_s, conv_s, delta_s = _trunk(x_sample, state_pool, state_conv, state_delta, PAST_LEN, p)
    return (y_p, y_s, pool_p, conv_p, delta_p, pool_s, conv_s, delta_s)
```

```python
import functools
from typing import NamedTuple

import jax
import jax.numpy as jnp
from jax import lax
from jax.experimental import pallas as pl
from jax.experimental.pallas import tpu as pltpu

F32 = jnp.float32
BF16 = jnp.bfloat16

EPS = 1e-6
PAST_LEN = 16384
POOL_WINDOWS = (2, 4, 8, 16)
POOL_PAST = max(POOL_WINDOWS) - 1
HEAD_DIM = 128
CONV_W = 4
CONV_PAST = CONV_W - 1
CHUNK = 64
SHORT_GROUP = 8
INV_BASE = 16
INTRA_UNITS = 16
INTER_PROBLEMS = 32

VMEM_LIMIT = 56 * 1024 * 1024
TOKEN_TILE = 512
FFN_SUBTILES = 2


def _params(*sem):
    return pltpu.CompilerParams(dimension_semantics=sem, vmem_limit_bytes=VMEM_LIMIT)


class _Res(NamedTuple):
    arr: jax.Array
    lead: tuple = ()

    @property
    def spec(self):
        n = len(self.lead)
        index = tuple(self.lead) + (0,) * (self.arr.ndim - n)
        return pl.BlockSpec((None,) * n + tuple(self.arr.shape[n:]), lambda *_: index,
                            pipeline_mode=pl.Buffered(1))


def _rms(x, g):
    return x * lax.rsqrt(jnp.mean(x * x, axis=-1, keepdims=True) + EPS) * g


def _silu(x):
    return x / (1.0 + jnp.exp(-x))


def _dot(a, b):
    return jnp.dot(a.astype(BF16), b.astype(BF16), preferred_element_type=F32)


def _dot_nt(a, b):
    return lax.dot_general(a.astype(BF16), b.astype(BF16), (((1,), (1,)), ((), ())),
                           preferred_element_type=F32)


def _ffn_kernel(x_ref, ng_ref, wg_ref, wu_ref, wd_ref, o_ref, *, pre, post, tf, n_sub):
    tm = x_ref.shape[0] // n_sub
    xs = [x_ref[i * tm:(i + 1) * tm, :] for i in range(n_sub)]
    hs = [_rms(x, ng_ref[pre:pre + 1, :]).astype(BF16) for x in xs]
    accs = [jnp.zeros(x.shape, F32) for x in xs]
    for c in range(wg_ref.shape[1] // tf):
        sl = slice(c * tf, (c + 1) * tf)
        for i in range(n_sub):
            gate = jnp.dot(hs[i], wg_ref[:, sl], preferred_element_type=F32)
            up = jnp.dot(hs[i], wu_ref[:, sl], preferred_element_type=F32)
            accs[i] = accs[i] + _dot(_silu(gate) * up, wd_ref[sl, :])
    for i in range(n_sub):
        o_ref[i * tm:(i + 1) * tm, :] = xs[i] + 0.5 * _rms(accs[i], ng_ref[post:post + 1, :])


def _ffn(x, ng, wg, wu, wd, pre, post):
    t, d = x.shape
    n_sub = FFN_SUBTILES if t % (FFN_SUBTILES * TOKEN_TILE) == 0 else 1
    tm = min(n_sub * TOKEN_TILE, t)
    kern = functools.partial(_ffn_kernel, pre=pre, post=post, tf=256, n_sub=n_sub)
    return pl.pallas_call(
        kern,
        grid=(t // tm,),
        in_specs=[pl.BlockSpec((tm, d), lambda i: (i, 0)), ng.spec, wg.spec, wu.spec, wd.spec],
        out_specs=pl.BlockSpec((tm, d), lambda i: (i, 0)),
        out_shape=jax.ShapeDtypeStruct((t, d), F32),
        compiler_params=_params("parallel"),
        name="ffn",
    )(x, ng.arr, wg.arr, wu.arr, wd.arr)


def _pool_seq_kernel(x_ref, past_ref, ng_ref, w_ref, sc_ref, o_ref, np_ref, hp_ref,
                     *, ts, pos0, pre, post):
    s = pl.program_id(1)
    off = POOL_PAST + 1
    d = x_ref.shape[-1]
    gw = d // len(POOL_WINDOWS)

    @pl.when(s == 0)
    def _():
        hp_ref[0:1, :] = jnp.zeros((1, d), F32)
        hp_ref[1:off, :] = past_ref[0]

    x = x_ref[0]
    h = _rms(x, ng_ref[pre:pre + 1, :])
    hp_ref[off:off + ts, :] = h
    pos = pos0 + s * ts + lax.broadcasted_iota(jnp.int32, (ts, 1), 0)
    outs = []
    for gi, w in enumerate(POOL_WINDOWS):
        lo, hi = gi * gw, (gi + 1) * gw
        hg = h[:, lo:hi]
        win = hg
        for k in range(1, w):
            win = win + hp_ref[off - k:off - k + ts, lo:hi]
        cnt = jnp.minimum(pos + 1, w).astype(F32)
        outs.append(_dot(win / cnt - hg, w_ref[gi]))
    y = jnp.concatenate(outs, axis=-1) * sc_ref[...]
    o_ref[0] = x + _rms(y, ng_ref[post:post + 1, :])
    tail = hp_ref[ts + 1:ts + off, :]
    np_ref[0] = tail
    hp_ref[1:off, :] = tail


def _pool_seq(x, past, ng, w, sc, pos0, pre, post):
    b, s, d = x.shape
    ts = min(TOKEN_TILE, s)
    kern = functools.partial(_pool_seq_kernel, ts=ts, pos0=pos0, pre=pre, post=post)
    return pl.pallas_call(
        kern,
        grid=(b, s // ts),
        in_specs=[pl.BlockSpec((1, ts, d), lambda i, j: (i, j, 0)),
                  pl.BlockSpec((1, POOL_PAST, d), lambda i, j: (i, 0, 0)),
                  ng.spec, w.spec, sc.spec],
        out_specs=[pl.BlockSpec((1, ts, d), lambda i, j: (i, j, 0)),
                   pl.BlockSpec((1, POOL_PAST, d), lambda i, j: (i, 0, 0))],
        out_shape=[jax.ShapeDtypeStruct((b, s, d), F32),
                   jax.ShapeDtypeStruct((b, POOL_PAST, d), F32)],
        scratch_shapes=[pltpu.VMEM((POOL_PAST + 1 + ts, d), F32)],
        compiler_params=_params("parallel", "arbitrary"),
        name="pool_seq",
    )(x, past, ng.arr, w.arr, sc.arr)


def _pool_step_kernel(x_ref, past_ref, ng_ref, w_ref, sc_ref, o_ref, np_ref, *, pos0, pre, post):
    steps, bb, d = x_ref.shape
    gw = d // len(POOL_WINDOWS)
    xs = [x_ref[t] for t in range(steps)]
    hs = [_rms(x, ng_ref[pre:pre + 1, :]) for x in xs]
    hp = [past_ref[p] for p in range(POOL_PAST)] + hs
    outs = []
    for gi, w in enumerate(POOL_WINDOWS):
        lo, hi = gi * gw, (gi + 1) * gw
        rows = []
        for t in range(steps):
            win = hp[POOL_PAST + t][:, lo:hi]
            for k in range(1, w):
                win = win + hp[POOL_PAST + t - k][:, lo:hi]
            cnt = float(min(pos0 + t + 1, w))
            rows.append(win / cnt - hs[t][:, lo:hi])
        outs.append(_dot(jnp.concatenate(rows, axis=0), w_ref[gi]))
    y = jnp.concatenate(outs, axis=-1) * sc_ref[...]
    for t in range(steps):
        o_ref[t] = xs[t] + _rms(y[t * bb:(t + 1) * bb], ng_ref[post:post + 1, :])
    for p in range(POOL_PAST):
        np_ref[p] = hp[steps + p]


def _pool_step(x, past, ng, w, sc, pos0, pre, post):
    steps, b, d = x.shape
    bb = min(32, b)
    kern = functools.partial(_pool_step_kernel, pos0=pos0, pre=pre, post=post)
    return pl.pallas_call(
        kern,
        grid=(b // bb,),
        in_specs=[pl.BlockSpec((steps, bb, d), lambda i: (0, i, 0)),
                  pl.BlockSpec((POOL_PAST, bb, d), lambda i: (0, i, 0)),
                  ng.spec, w.spec, sc.spec],
        out_specs=[pl.BlockSpec((steps, bb, d), lambda i: (0, i, 0)),
                   pl.BlockSpec((POOL_PAST, bb, d), lambda i: (0, i, 0))],
        out_shape=[jax.ShapeDtypeStruct((steps, b, d), F32),
                   jax.ShapeDtypeStruct((POOL_PAST, b, d), F32)],
        compiler_params=_params("parallel"),
        name="pool_step",
    )(x, past, ng.arr, w.arr, sc.arr)


def _dn_proj_kernel(x_ref, ng_ref, w_ref, alog_ref, dtb_ref, hm_ref, beta_ref, g_ref, *, pre, tn):
    h = _rms(x_ref[...], ng_ref[pre:pre + 1, :]).astype(BF16)
    n_blk = hm_ref.shape[0]
    hv = beta_ref.shape[1]
    per = tn // HEAD_DIM
    for c in range(n_blk // per):
        res = jnp.dot(h, w_ref[:, c * tn:(c + 1) * tn], preferred_element_type=F32)
        for m in range(per):
            hm_ref[c * per + m] = res[:, m * HEAD_DIM:(m + 1) * HEAD_DIM]
    gates = jnp.dot(h, w_ref[:, n_blk * HEAD_DIM:], preferred_element_type=F32)
    b = gates[:, :hv]
    a = gates[:, hv:] + dtb_ref[...]
    beta_ref[...] = 1.0 / (1.0 + jnp.exp(-b))
    softplus = jnp.maximum(a, 0.0) + jnp.log1p(jnp.exp(-jnp.abs(a)))
    g_ref[...] = -jnp.exp(alog_ref[...]) * softplus


def _dn_proj(x, ng, w_in, a_log, dt_bias, pre):
    t, d = x.shape
    tm = min(TOKEN_TILE, t)
    hv = a_log.arr.shape[-1]
    n_blk = (w_in.arr.shape[-1] - 2 * hv) // HEAD_DIM
    kern = functools.partial(_dn_proj_kernel, pre=pre, tn=512)
    row = lambda n: pl.BlockSpec((tm, n), lambda i: (i, 0))
    return pl.pallas_call(
        kern,
        grid=(t // tm,),
        in_specs=[row(d), ng.spec, w_in.spec, a_log.spec, dt_bias.spec],
        out_specs=[pl.BlockSpec((n_blk, tm, HEAD_DIM), lambda i: (0, i, 0)), row(hv), row(hv)],
        out_shape=[jax.ShapeDtypeStruct((n_blk, t, HEAD_DIM), F32),
                   jax.ShapeDtypeStruct((t, hv), F32), jax.ShapeDtypeStruct((t, hv), F32)],
        compiler_params=_params("parallel"),
        name="dn_proj",
    )(x, ng.arr, w_in.arr, a_log.arr, dt_bias.arr)


def _dot3_each(pairs):
    split = lambda x: (x.astype(BF16), (x - x.astype(BF16).astype(F32)).astype(BF16))
    parts = [(split(a), split(b)) for a, b in pairs]
    mm = lambda x, y: jnp.dot(x, y, preferred_element_type=F32)
    hh = [mm(a[0], b[0]) for a, b in parts]
    hl = [mm(a[0], b[1]) for a, b in parts]
    lh = [mm(a[1], b[0]) for a, b in parts]
    return [x + (y + z) for x, y, z in zip(hh, hl, lh)]


def _unit_lower_inverse_each(l_mats, c):
    row = lax.broadcasted_iota(jnp.int32, (c, c), 0)
    col = lax.broadcasted_iota(jnp.int32, (c, c), 1)
    blk = jnp.bitwise_xor(row, col)
    eye = jnp.where(row == col, 1.0, 0.0)
    bs = min(INV_BASE, c)
    ps = [jnp.where(blk < bs, -l, 0.0) for l in l_mats]
    ts = [eye + p for p in ps]
    n = 2
    while n < bs:
        ps = [_dot(p, p) for p in ps]
        tps = [_dot(t, p) for t, p in zip(ts, ps)]
        ts = [t + tp for t, tp in zip(ts, tps)]
        n *= 2
    while bs < c:
        es = [jnp.where((blk < 2 * bs) & (blk >= bs), l, 0.0) for l in l_mats]
        tes = [_dot(t, e) for t, e in zip(ts, es)]
        tets = [_dot(te, t) for te, t in zip(tes, ts)]
        ts = [t - tet for t, tet in zip(ts, tets)]
        bs *= 2
    lts = _dot3_each(list(zip(l_mats, ts)))
    trs = [_dot(t, eye - t - lt) for t, lt in zip(ts, lts)]
    return [t + tr for t, tr in zip(ts, trs)]


def _dn_seq_kernel(stack_ref, hm_ref, beta_ref, g_ref, cp_ref, cw_ref, s0_ref, nw_ref,
                   o_ref, sn_ref,
                   raw_ref, u_ref, wq_ref, attn_ref, kdt_ref, egl_ref, *, c, bb, n_kh):
    step = pl.program_id(1)
    dk = HEAD_DIM
    hv = 2 * n_kh
    n_conv = 4 * n_kh
    rows_per_step = hm_ref.shape[1]
    n_chunks = rows_per_step // c
    halo = CONV_PAST * bb
    off = raw_ref.shape[1] - rows_per_step

    @pl.when(step == 0)
    def _():
        raw_ref[:, off - halo:off, :] = cp_ref[0]
        sn_ref[...] = s0_ref[...]

    @pl.when(step > 0)
    def _():
        raw_ref[:, off - halo:off, :] = raw_ref[:, off + rows_per_step - halo:off + rows_per_step, :]

    raw_ref[:, off:off + rows_per_step, :] = hm_ref[0:n_conv]

    row = lax.broadcasted_iota(jnp.int32, (c, c), 0)
    col = lax.broadcasted_iota(jnp.int32, (c, c), 1)
    same = (jnp.bitwise_xor(row, col) & (bb - 1)) == 0
    low_eq = same & (row >= col)
    low = same & (row > col)
    up_eq = same & (row <= col)
    last = same & (col >= c - bb)
    eye = row == col
    head_lane = lax.broadcasted_iota(jnp.int32, (1, hv), 1)
    member = lax.broadcasted_iota(jnp.int32, (c, 1), 0) & (bb - 1)

    def conv(cb, r0):
        acc = None
        for j in range(CONV_W):
            lo = off + r0 - (CONV_PAST - j) * bb
            term = raw_ref[cb, lo:lo + c, :] * cw_ref[cb, j:j + 1, :]
            acc = term if acc is None else acc + term
        return _silu(acc)


    def intra(units):
        qs, ks = [], []
        for hk, ci in units:
            qc = conv(hk, ci * c)
            kc = conv(n_kh + hk, ci * c)
            qs.append(qc * lax.rsqrt(jnp.sum(qc * qc, axis=-1, keepdims=True) + EPS) * (dk ** -0.5))
            ks.append(kc * lax.rsqrt(jnp.sum(kc * kc, axis=-1, keepdims=True) + EPS))
        kks = [_dot_nt(k, k) for k in ks]
        qks = [_dot_nt(q, k) for q, k in zip(qs, ks)]
        probs = []
        for (hk, ci), q, k, kk, qk in zip(units, qs, ks, kks, qks):
            r0 = ci * c
            beta_blk = beta_ref[r0:r0 + c, :]
            g_blk = g_ref[r0:r0 + c, :]
            for j in range(2):
                h = 2 * hk + j
                sel = head_lane == h
                beta = jnp.sum(jnp.where(sel, beta_blk, 0.0), axis=1, keepdims=True)
                g = jnp.sum(jnp.where(sel, g_blk, 0.0), axis=1, keepdims=True)
                g_row = jnp.sum(jnp.where(eye, g, 0.0), axis=0, keepdims=True)
                gc = jnp.sum(jnp.where(low_eq, g_row, 0.0), axis=1, keepdims=True)
                gc_row = jnp.sum(jnp.where(up_eq, g, 0.0), axis=0, keepdims=True)
                g_last = jnp.sum(jnp.where(last, gc_row, 0.0), axis=1, keepdims=True)
                decay = jnp.exp(jnp.where(low_eq, gc - gc_row, -jnp.inf))
                egc = jnp.exp(gc)
                attn_ref[h, r0:r0 + c, :] = (qk * decay).astype(BF16)
                wq_ref[h, ci, c:2 * c, :] = (q * egc).astype(BF16)
                kdt_ref[h, ci] = (k * jnp.exp(g_last - gc)).T.astype(BF16)
                egl_ref[h, ci] = jnp.broadcast_to(egc[c - bb:c, :], (bb, dk))
                v = conv(2 * n_kh + h, r0)
                rhs = jnp.concatenate([v * beta, k * (beta * egc)], axis=1)
                probs.append((h, ci, jnp.where(low, kk * beta * decay, 0.0), rhs))
        t_mats = _unit_lower_inverse_each([p[2] for p in probs], c)
        uws = [_dot(t, p[3]) for t, p in zip(t_mats, probs)]
        for (h, ci, _, _), uw in zip(probs, uws):
            u_ref[h, ci * c:(ci + 1) * c, :] = uw[:, :dk]
            wq_ref[h, ci, 0:c, :] = uw[:, dk:].astype(BF16)

    kh_per_iter = min(n_kh, max(1, INTRA_UNITS // n_chunks))

    def intra_body(it, carry):
        intra([(it * kh_per_iter + m, ci) for m in range(kh_per_iter) for ci in range(n_chunks)])
        return carry

    lax.fori_loop(0, n_kh // kh_per_iter, intra_body, 0)

    def inter(ci, heads):
        rows = pl.ds(pl.multiple_of(ci * c, c), c)
        mm = lambda x, y: jnp.dot(x, y, preferred_element_type=F32)
        res = [[mm(wq_ref[h, ci], sn_ref[s, h].astype(BF16)) for s in range(bb)] for h in heads]
        v_news, o_states = [], []
        for h, res_h in zip(heads, res):
            v_new = u_ref[h, rows, :]
            o_state = None
            for s, r in enumerate(res_h):
                if bb == 1:
                    v_new = v_new - r[:c]
                    o_state = r[c:]
                else:
                    mine = member == s
                    v_new = v_new - jnp.where(mine, r[:c], 0.0)
                    o_state = jnp.where(mine, r[c:], 0.0 if o_state is None else o_state)
            v_news.append(v_new)
            o_states.append(o_state)
        v_seq = [[(v if bb == 1 else jnp.where(member == s, v, 0.0)).astype(BF16) for s in range(bb)]
                 for v in v_news]
        attn_v = [mm(attn_ref[h, rows, :], v.astype(BF16)) for h, v in zip(heads, v_news)]
        upd = [[mm(kdt_ref[h, ci], v_s) for v_s in v_h] for h, v_h in zip(heads, v_seq)]
        for h, o_state, av, upd_h in zip(heads, o_states, attn_v, upd):
            for s in range(bb):
                sn_ref[s, h] = sn_ref[s, h] * egl_ref[h, ci, s:s + 1, :] + upd_h[s]
            z = hm_ref[n_conv + h, rows, :]
            o_ref[h, rows, :] = (_rms(o_state + av, nw_ref[...]) * _silu(z)).astype(BF16)

    heads_per_iter = min(hv, max(1, INTER_PROBLEMS // bb))

    def inter_body(ci, carry):
        if heads_per_iter == hv:
            inter(ci, list(range(hv)))
        else:
            def head_body(it, carry2):
                inter(ci, [it * heads_per_iter + m for m in range(heads_per_iter)])
                return carry2
            lax.fori_loop(0, hv // heads_per_iter, head_body, 0)
        return carry

    lax.fori_loop(0, n_chunks, inter_body, 0)


def _dn_seq(hm, beta, g, conv_past, conv_w, s0_all, layer, stack, norm_w, c, bb, rows_per_step):
    n_blk, t, dk = hm.shape
    n_grp = s0_all.shape[1] // bb
    hv = s0_all.shape[2]
    n_kh = hv // 2
    n_conv = 4 * n_kh
    rows_per_grp = t // n_grp
    steps = rows_per_grp // rows_per_step
    n_chunks = rows_per_step // c
    halo = CONV_PAST * bb
    off = -(-halo // 8) * 8
    assert n_blk == n_conv + hv and rows_per_step % c == 0 and rows_per_grp % rows_per_step == 0
    assert bb & (bb - 1) == 0 and c & (c - 1) == 0 and c % bb == 0
    kern = functools.partial(_dn_seq_kernel, c=c, bb=bb, n_kh=n_kh)
    tok = lambda i, j: i * steps + j
    state = pl.BlockSpec((None, bb, hv, dk, dk), lambda i, j: (layer, i, 0, 0, 0))
    gate = pl.BlockSpec((rows_per_step, hv), lambda i, j: (tok(i, j), 0))
    return pl.pallas_call(
        kern,
        grid=(n_grp, steps),
        in_specs=[pl.BlockSpec(memory_space=pl.ANY),
                  pl.BlockSpec((n_blk, rows_per_step, dk), lambda i, j: (0, tok(i, j), 0)),
                  gate, gate,
                  pl.BlockSpec((1, n_conv, halo, dk), lambda i, j: (i, 0, 0, 0)),
                  conv_w.spec, state, norm_w.spec],
        out_specs=[pl.BlockSpec((hv, rows_per_step, dk), lambda i, j: (0, tok(i, j), 0)), state],
        out_shape=[jax.ShapeDtypeStruct((hv, t, dk), BF16), jax.ShapeDtypeStruct(s0_all.shape, F32)],
        scratch_shapes=[pltpu.VMEM((n_conv, off + rows_per_step, dk), F32),
                        pltpu.VMEM((hv, rows_per_step, dk), F32),
                        pltpu.VMEM((hv, n_chunks, 2 * c, dk), BF16),
                        pltpu.VMEM((hv, rows_per_step, c), BF16),
                        pltpu.VMEM((hv, n_chunks, dk, c), BF16),
                        pltpu.VMEM((hv, n_chunks, bb, dk), F32)],
        input_output_aliases={} if stack is None else {0: 1},
        compiler_params=_params("parallel", "arbitrary"),
        name="dn_seq",
    )(s0_all if stack is None else stack, hm, beta, g, conv_past, conv_w.arr, s0_all, norm_w.arr)


def _dn_out_kernel(o_ref, x_ref, ng_ref, w_ref, y_ref, *, post):
    o = jnp.concatenate([o_ref[h] for h in range(o_ref.shape[0])], axis=1)
    y = _dot(o, w_ref[...])
    y_ref[...] = x_ref[...] + _rms(y, ng_ref[post:post + 1, :])


def _dn_out(o, x, ng, w_out, post):
    t, d = x.shape
    hv, _, dk = o.shape
    tm = min(TOKEN_TILE, t)
    kern = functools.partial(_dn_out_kernel, post=post)
    return pl.pallas_call(
        kern,
        grid=(t // tm,),
        in_specs=[pl.BlockSpec((hv, tm, dk), lambda i: (0, i, 0)),
                  pl.BlockSpec((tm, d), lambda i: (i, 0)), ng.spec, w_out.spec],
        out_specs=pl.BlockSpec((tm, d), lambda i: (i, 0)),
        out_shape=jax.ShapeDtypeStruct((t, d), F32),
        compiler_params=_params("parallel"),
        name="dn_out",
    )(o, x, ng.arr, w_out.arr)


def _dn_layer(x, conv_past, s0_all, layer, stack, ng, w_in, a_log, dt_bias, conv_w, norm_w, w_out):
    b, s, d = x.shape
    dk = HEAD_DIM
    conv_dim = conv_past.shape[-1]
    n_conv = conv_dim // dk
    assert s >= CONV_PAST
    if s >= CHUNK:
        bb, c, rows_per_step = 1, CHUNK, min(s, 4 * CHUNK)
    else:
        bb = min(SHORT_GROUP, b)
        c = rows_per_step = s * bb
    n_grp = b // bb
    xt = x.reshape(n_grp, bb, s, d).transpose(0, 2, 1, 3).reshape(b * s, d)
    hm, beta, g = _dn_proj(xt, ng, w_in, a_log, dt_bias, pre=2)
    cp = conv_past.reshape(n_grp, bb, CONV_PAST, n_conv, dk).transpose(0, 3, 2, 1, 4)
    cp = cp.reshape(n_grp, n_conv, CONV_PAST * bb, dk)
    o, stack = _dn_seq(hm, beta, g, cp, conv_w, s0_all, layer, stack, norm_w, c, bb, rows_per_step)
    y = _dn_out(o, xt, ng, w_out, post=3)
    y = y.reshape(n_grp, s, bb, d).transpose(0, 2, 1, 3).reshape(b, s, d)
    tail = hm.reshape(-1, n_grp, s, bb, dk)[:n_conv, :, s - CONV_PAST:]
    new_conv = tail.transpose(1, 3, 2, 0, 4).reshape(b, CONV_PAST, conv_dim)
    return y, new_conv, stack


def _trunk(x, st_pool, st_conv, st_delta, pos0, p):
    b, s, d = x.shape
    depth = p["wg"].shape[0]
    new_pool, new_conv, new_delta = [], [], None
    for i in range(depth):
        ng = _Res(p["ng"], (i,))
        ffn_w = lambda half: [_Res(p[k], (i, half)) for k in ("wg", "wu", "wd")]
        x = _ffn(x.reshape(b * s, d), ng, *ffn_w(0), 0, 1).reshape(b, s, d)
        j = i // 2
        if i % 2 == 0:
            pool_w, pool_sc = _Res(p["pool_w"], (j,)), _Res(p["pool_scale"], (j,))
            if s >= 2 * POOL_PAST:
                x, ps = _pool_seq(x, st_pool[j], ng, pool_w, pool_sc, pos0, 2, 3)
            else:
                xt, pt = _pool_step(x.transpose(1, 0, 2), st_pool[j].transpose(1, 0, 2), ng,
                                    pool_w, pool_sc, pos0, 2, 3)
                x, ps = xt.transpose(1, 0, 2), pt.transpose(1, 0, 2)
            new_pool.append(ps)
        else:
            x, cs, new_delta = _dn_layer(
                x, st_conv[j], st_delta, j, new_delta, ng, _Res(p["w_in"], (j,)),
                _Res(p["a_log"], (j,)), _Res(p["dt_bias"], (j,)), _Res(p["conv_w"], (j,)),
                _Res(p["norm_w"], (j,)), _Res(p["w_out"], (j,)))
            new_conv.append(cs)
        x = _ffn(x.reshape(b * s, d), ng, *ffn_w(1), 4, 5).reshape(b, s, d)
    return x, jnp.stack(new_pool), jnp.stack(new_conv), new_delta


def kernel(x_prompt, x_sample, state_pool, state_conv, state_delta, norm_gains, w_ffn_gate,
           w_ffn_up, w_ffn_down, pool_w, pool_scale, dn_w_in, dn_conv_w, dn_a_log, dn_dt_bias,
           dn_norm_w, dn_w_out):
    n_dn, _, _, conv_dim = state_conv.shape
    hv = state_delta.shape[2]
    n_pool = state_pool.shape[0]
    d = x_prompt.shape[-1]
    n_conv = conv_dim // HEAD_DIM
    p = {
        "ng": norm_gains,
        "wg": w_ffn_gate.astype(BF16), "wu": w_ffn_up.astype(BF16), "wd": w_ffn_down.astype(BF16),
        "pool_w": pool_w.astype(BF16), "pool_scale": pool_scale.reshape(n_pool, 1, d),
        "w_in": dn_w_in.astype(BF16),
        "a_log": dn_a_log.reshape(n_dn, 1, hv), "dt_bias": dn_dt_bias.reshape(n_dn, 1, hv),
        "conv_w": dn_conv_w.reshape(n_dn, CONV_W, n_conv, HEAD_DIM).transpose(0, 2, 1, 3),
        "norm_w": dn_norm_w.reshape(n_dn, 1, HEAD_DIM),
        "w_out": dn_w_out.astype(BF16),
    }
    bp = x_prompt.shape[0]
    zp = jnp.zeros((n_pool, bp) + state_pool.shape[2:], F32)
    zc = jnp.zeros((n_dn, bp) + state_conv.shape[2:], F32)
    zd = jnp.zeros((n_dn, bp) + state_delta.shape[2:], F32)
    y_p, pool_p, conv_p, delta_p = _trunk(x_prompt, zp, zc, zd, 0, p)
    y_s, pool_s, conv_s, delta_s = _trunk(x_sample, state_pool, state_conv, state_delta, PAST_LEN, p)
    return (y_p, y_s, pool_p, conv_p, delta_p, pool_s, conv_s, delta_s)
```

```python
import functools
from typing import NamedTuple

import jax
import jax.numpy as jnp
from jax import lax
from jax.experimental import pallas as pl
from jax.experimental.pallas import tpu as pltpu

F32 = jnp.float32
BF16 = jnp.bfloat16

EPS = 1e-6
PAST_LEN = 16384
POOL_WINDOWS = (2, 4, 8, 16)
POOL_PAST = max(POOL_WINDOWS) - 1
HEAD_DIM = 128
CONV_W = 4
CONV_PAST = CONV_W - 1
CHUNK = 64
SHORT_GROUP = 8
INV_BASE = 16
INTRA_UNITS = 16
INTER_PROBLEMS = 32
CONV_PIECE = 128

VMEM_LIMIT = 56 * 1024 * 1024
TOKEN_TILE = 512
FFN_SUBTILES = 2
FFN_CHUNK = 256
FFN_STAGE_SLOTS = 2


def _params(*sem):
    return pltpu.CompilerParams(dimension_semantics=sem, vmem_limit_bytes=VMEM_LIMIT)


class _Res(NamedTuple):
    arr: jax.Array
    lead: tuple = ()

    @property
    def spec(self):
        n = len(self.lead)
        index = tuple(self.lead) + (0,) * (self.arr.ndim - n)
        return pl.BlockSpec((None,) * n + tuple(self.arr.shape[n:]), lambda *_: index,
                            pipeline_mode=pl.Buffered(1))


def _rms(x, g):
    return x * lax.rsqrt(jnp.mean(x * x, axis=-1, keepdims=True) + EPS) * g


def _silu(x):
    return x / (1.0 + jnp.exp(-x))


def _dot(a, b):
    return jnp.dot(a.astype(BF16), b.astype(BF16), preferred_element_type=F32)


def _dot_nt(a, b):
    return lax.dot_general(a.astype(BF16), b.astype(BF16), (((1,), (1,)), ((), ())),
                           preferred_element_type=F32)


def _ffn_kernel(x_ref, ng_ref, wg_hbm, wu_hbm, wd_hbm, o_ref,
                wg_ref, wu_ref, wd_ref, sg_ref, su_ref, sd_ref, h_ref, acc_ref, sem,
                *, layer, half, pre, post, tf, n_sub):
    step = pl.program_id(0)
    n_chunks = wg_ref.shape[1] // tf
    n_slots = sg_ref.shape[0]

    def copies(c):
        slot = c % n_slots
        cols = pl.ds(c * tf, tf)
        return (pltpu.make_async_copy(wg_hbm.at[layer, half, :, cols], sg_ref.at[slot], sem.at[0, slot]),
                pltpu.make_async_copy(wu_hbm.at[layer, half, :, cols], su_ref.at[slot], sem.at[1, slot]),
                pltpu.make_async_copy(wd_hbm.at[layer, half, cols, :], sd_ref.at[slot], sem.at[2, slot]))

    def fetch(c):
        slot = c % n_slots
        sl = slice(c * tf, (c + 1) * tf)
        for cp in copies(c):
            cp.wait()
        wg_ref[:, sl] = sg_ref[slot].astype(BF16)
        wu_ref[:, sl] = su_ref[slot].astype(BF16)
        wd_ref[sl, :] = sd_ref[slot].astype(BF16)
        if c + n_slots < n_chunks:
            for cp in copies(c + n_slots):
                cp.start()

    def compute(before_chunk):
        tm = x_ref.shape[0] // n_sub
        rows = [slice(i * tm, (i + 1) * tm) for i in range(n_sub)]
        for r in rows:
            h_ref[r, :] = _rms(x_ref[r, :], ng_ref[pre:pre + 1, :]).astype(BF16)
        for c in range(n_chunks):
            before_chunk(c)
            sl = slice(c * tf, (c + 1) * tf)
            for r in rows:
                gate = jnp.dot(h_ref[r, :], wg_ref[:, sl], preferred_element_type=F32)
                up = jnp.dot(h_ref[r, :], wu_ref[:, sl], preferred_element_type=F32)
                part = _dot(_silu(gate) * up, wd_ref[sl, :])
                acc_ref[r, :] = part if c == 0 else acc_ref[r, :] + part
        for r in rows:
            o_ref[r, :] = x_ref[r, :] + 0.5 * _rms(acc_ref[r, :], ng_ref[post:post + 1, :])

    @pl.when(step == 0)
    def _():
        for c in range(min(n_slots, n_chunks)):
            for cp in copies(c):
                cp.start()
        compute(fetch)

    @pl.when(step > 0)
    def _():
        compute(lambda c: None)


def _ffn(x, ng, wg, wu, wd, layer, half):
    t, d = x.shape
    d_ff = wg.shape[-1]
    tf = FFN_CHUNK
    n_sub = FFN_SUBTILES if t % (FFN_SUBTILES * TOKEN_TILE) == 0 else 1
    tm = min(n_sub * TOKEN_TILE, t)
    kern = functools.partial(_ffn_kernel, layer=layer, half=half, pre=4 * half, post=4 * half + 1,
                             tf=tf, n_sub=n_sub)
    hbm = pl.BlockSpec(memory_space=pl.ANY)
    return pl.pallas_call(
        kern,
        grid=(t // tm,),
        in_specs=[pl.BlockSpec((tm, d), lambda i: (i, 0)), ng.spec, hbm, hbm, hbm],
        out_specs=pl.BlockSpec((tm, d), lambda i: (i, 0)),
        out_shape=jax.ShapeDtypeStruct((t, d), F32),
        scratch_shapes=[pltpu.VMEM((d, d_ff), BF16), pltpu.VMEM((d, d_ff), BF16), pltpu.VMEM((d_ff, d), BF16),
                        pltpu.VMEM((FFN_STAGE_SLOTS, d, tf), F32), pltpu.VMEM((FFN_STAGE_SLOTS, d, tf), F32),
                        pltpu.VMEM((FFN_STAGE_SLOTS, tf, d), F32),
                        pltpu.VMEM((tm, d), BF16), pltpu.VMEM((tm, d), F32),
                        pltpu.SemaphoreType.DMA((3, FFN_STAGE_SLOTS))],
        compiler_params=_params("arbitrary"),
        name="ffn",
    )(x, ng.arr, wg, wu, wd)


def _pool_seq_kernel(x_ref, past_ref, ng_ref, w_ref, sc_ref, o_ref, np_ref, hp_ref,
                     *, ts, pos0, pre, post):
    s = pl.program_id(1)
    off = POOL_PAST + 1
    d = x_ref.shape[-1]
    gw = d // len(POOL_WINDOWS)

    @pl.when(s == 0)
    def _():
        hp_ref[0:1, :] = jnp.zeros((1, d), F32)
        hp_ref[1:off, :] = past_ref[0]

    x = x_ref[0]
    h = _rms(x, ng_ref[pre:pre + 1, :])
    hp_ref[off:off + ts, :] = h
    pos = pos0 + s * ts + lax.broadcasted_iota(jnp.int32, (ts, 1), 0)
    outs = []
    for gi, w in enumerate(POOL_WINDOWS):
        lo, hi = gi * gw, (gi + 1) * gw
        hg = h[:, lo:hi]
        win = hg
        for k in range(1, w):
            win = win + hp_ref[off - k:off - k + ts, lo:hi]
        cnt = jnp.minimum(pos + 1, w).astype(F32)
        outs.append(_dot(win / cnt - hg, w_ref[gi]))
    y = jnp.concatenate(outs, axis=-1) * sc_ref[...]
    o_ref[0] = x + _rms(y, ng_ref[post:post + 1, :])
    tail = hp_ref[ts + 1:ts + off, :]
    np_ref[0] = tail
    hp_ref[1:off, :] = tail


def _pool_seq(x, past, ng, w, sc, pos0, pre, post):
    b, s, d = x.shape
    ts = min(TOKEN_TILE, s)
    kern = functools.partial(_pool_seq_kernel, ts=ts, pos0=pos0, pre=pre, post=post)
    return pl.pallas_call(
        kern,
        grid=(b, s // ts),
        in_specs=[pl.BlockSpec((1, ts, d), lambda i, j: (i, j, 0)),
                  pl.BlockSpec((1, POOL_PAST, d), lambda i, j: (i, 0, 0)),
                  ng.spec, w.spec, sc.spec],
        out_specs=[pl.BlockSpec((1, ts, d), lambda i, j: (i, j, 0)),
                   pl.BlockSpec((1, POOL_PAST, d), lambda i, j: (i, 0, 0))],
        out_shape=[jax.ShapeDtypeStruct((b, s, d), F32),
                   jax.ShapeDtypeStruct((b, POOL_PAST, d), F32)],
        scratch_shapes=[pltpu.VMEM((POOL_PAST + 1 + ts, d), F32)],
        compiler_params=_params("parallel", "arbitrary"),
        name="pool_seq",
    )(x, past, ng.arr, w.arr, sc.arr)


def _pool_step_kernel(x_ref, past_ref, ng_ref, w_ref, sc_ref, o_ref, np_ref, *, pos0, pre, post):
    steps, bb, d = x_ref.shape
    gw = d // len(POOL_WINDOWS)
    xs = [x_ref[t] for t in range(steps)]
    hs = [_rms(x, ng_ref[pre:pre + 1, :]) for x in xs]
    hp = [past_ref[p] for p in range(POOL_PAST)] + hs
    outs = []
    for gi, w in enumerate(POOL_WINDOWS):
        lo, hi = gi * gw, (gi + 1) * gw
        rows = []
        for t in range(steps):
            win = hp[POOL_PAST + t][:, lo:hi]
            for k in range(1, w):
                win = win + hp[POOL_PAST + t - k][:, lo:hi]
            cnt = float(min(pos0 + t + 1, w))
            rows.append(win / cnt - hs[t][:, lo:hi])
        outs.append(_dot(jnp.concatenate(rows, axis=0), w_ref[gi]))
    y = jnp.concatenate(outs, axis=-1) * sc_ref[...]
    for t in range(steps):
        o_ref[t] = xs[t] + _rms(y[t * bb:(t + 1) * bb], ng_ref[post:post + 1, :])
    for p in range(POOL_PAST):
        np_ref[p] = hp[steps + p]


def _pool_step(x, past, ng, w, sc, pos0, pre, post):
    steps, b, d = x.shape
    bb = min(32, b)
    kern = functools.partial(_pool_step_kernel, pos0=pos0, pre=pre, post=post)
    return pl.pallas_call(
        kern,
        grid=(b // bb,),
        in_specs=[pl.BlockSpec((steps, bb, d), lambda i: (0, i, 0)),
                  pl.BlockSpec((POOL_PAST, bb, d), lambda i: (0, i, 0)),
                  ng.spec, w.spec, sc.spec],
        out_specs=[pl.BlockSpec((steps, bb, d), lambda i: (0, i, 0)),
                   pl.BlockSpec((POOL_PAST, bb, d), lambda i: (0, i, 0))],
        out_shape=[jax.ShapeDtypeStruct((steps, b, d), F32),
                   jax.ShapeDtypeStruct((POOL_PAST, b, d), F32)],
        compiler_params=_params("parallel"),
        name="pool_step",
    )(x, past, ng.arr, w.arr, sc.arr)


def _dn_proj_kernel(x_ref, ng_ref, w_ref, alog_ref, dtb_ref, cp_ref, cw_ref,
                    hm_ref, beta_ref, g_ref, tail_ref, buf_ref, carry_ref,
                    *, pre, tn, n_kh, bb, tiles_per_grp):
    i = pl.program_id(0)
    tm = x_ref.shape[0]
    n_blk = hm_ref.shape[0]
    hv = 2 * n_kh
    n_conv = 4 * n_kh
    n_seg = cp_ref.shape[0]
    seg = tm // n_seg
    halo = CONV_PAST * bb
    off = buf_ref.shape[1] - seg
    per = tn // HEAD_DIM
    carried = tiles_per_grp > 1
    h = _rms(x_ref[...], ng_ref[pre:pre + 1, :]).astype(BF16)

    if carried:
        @pl.when(i % tiles_per_grp == 0)
        def _():
            carry_ref[...] = cp_ref[0]

    def slot_of(c, m, s):
        return ((c % 2) * per + m) * n_seg + s

    def stage(c, res):
        for m in range(per):
            cb = c * per + m
            blk = res[:, m * HEAD_DIM:(m + 1) * HEAD_DIM]
            if cb >= n_conv:
                hm_ref[cb] = blk
                continue
            for s in range(n_seg):
                slot = slot_of(c, m, s)
                buf_ref[slot, off - halo:off, :] = carry_ref[cb] if carried else cp_ref[s, cb]
                buf_ref[slot, off:off + seg, :] = blk[s * seg:(s + 1) * seg, :]
                tail = buf_ref[slot, off + seg - halo:off + seg, :]
                if carried:
                    carry_ref[cb] = tail
                tail_ref[s, cb] = tail

    def convolve(c):
        piece = min(seg, CONV_PIECE)
        for m in range(per):
            cb = c * per + m
            if cb >= n_conv:
                continue
            for s in range(n_seg):
                slot = slot_of(c, m, s)
                for r0 in range(0, seg, piece):
                    acc = None
                    for j in range(CONV_W):
                        lo = off - halo + j * bb + r0
                        term = buf_ref[slot, lo:lo + piece, :] * cw_ref[cb, j:j + 1, :]
                        acc = term if acc is None else acc + term
                    out = _silu(acc)
                    if cb < 2 * n_kh:
                        out = out * lax.rsqrt(jnp.sum(out * out, axis=-1, keepdims=True) + EPS)
                        if cb < n_kh:
                            out = out * (HEAD_DIM ** -0.5)
                    hm_ref[cb, s * seg + r0:s * seg + r0 + piece, :] = out

    n_chunks = n_blk // per
    for c in range(n_chunks):
        res = jnp.dot(h, w_ref[:, c * tn:(c + 1) * tn], preferred_element_type=F32)
        if c > 0:
            convolve(c - 1)
        stage(c, res)
    convolve(n_chunks - 1)
    gates = jnp.dot(h, w_ref[:, n_blk * HEAD_DIM:], preferred_element_type=F32)
    b = gates[:, :hv]
    a = gates[:, hv:] + dtb_ref[...]
    beta_ref[...] = 1.0 / (1.0 + jnp.exp(-b))
    softplus = jnp.maximum(a, 0.0) + jnp.log1p(jnp.exp(-jnp.abs(a)))
    g_ref[...] = -jnp.exp(alog_ref[...]) * softplus


def _dn_proj(x, ng, w_in, a_log, dt_bias, conv_past, conv_w, bb, pre):
    t, d = x.shape
    tm = min(TOKEN_TILE, t)
    hv = a_log.arr.shape[-1]
    n_blk = (w_in.arr.shape[-1] - 2 * hv) // HEAD_DIM
    n_grp, n_conv, halo, dk = conv_past.shape
    rows_per_grp = t // n_grp
    n_seg = max(1, tm // rows_per_grp)
    tiles_per_grp = max(1, rows_per_grp // tm)
    assert (rows_per_grp % tm == 0 or tm % rows_per_grp == 0) and tm // n_seg >= halo
    tn = 4 * HEAD_DIM
    kern = functools.partial(_dn_proj_kernel, pre=pre, tn=tn, n_kh=hv // 2, bb=bb,
                             tiles_per_grp=tiles_per_grp)
    row = lambda n: pl.BlockSpec((tm, n), lambda i: (i, 0))
    past = pl.BlockSpec((n_seg, n_conv, halo, dk), lambda i: (i // tiles_per_grp, 0, 0, 0))
    off = -(-halo // 8) * 8
    return pl.pallas_call(
        kern,
        grid=(t // tm,),
        in_specs=[row(d), ng.spec, w_in.spec, a_log.spec, dt_bias.spec, past, conv_w.spec],
        out_specs=[pl.BlockSpec((n_blk, tm, HEAD_DIM), lambda i: (0, i, 0)), row(hv), row(hv), past],
        out_shape=[jax.ShapeDtypeStruct((n_blk, t, HEAD_DIM), F32),
                   jax.ShapeDtypeStruct((t, hv), F32), jax.ShapeDtypeStruct((t, hv), F32),
                   jax.ShapeDtypeStruct(conv_past.shape, F32)],
        scratch_shapes=[pltpu.VMEM((2 * (tn // HEAD_DIM) * n_seg, off + tm // n_seg, dk), F32),
                        pltpu.VMEM((n_conv, halo, dk), F32)],
        compiler_params=_params("arbitrary"),
        name="dn_proj",
    )(x, ng.arr, w_in.arr, a_log.arr, dt_bias.arr, conv_past, conv_w.arr)


def _dot3_each(pairs):
    split = lambda x: (x.astype(BF16), (x - x.astype(BF16).astype(F32)).astype(BF16))
    parts = [(split(a), split(b)) for a, b in pairs]
    mm = lambda x, y: jnp.dot(x, y, preferred_element_type=F32)
    hh = [mm(a[0], b[0]) for a, b in parts]
    hl = [mm(a[0], b[1]) for a, b in parts]
    lh = [mm(a[1], b[0]) for a, b in parts]
    return [x + (y + z) for x, y, z in zip(hh, hl, lh)]


def _unit_lower_inverse_each(l_mats, c):
    row = lax.broadcasted_iota(jnp.int32, (c, c), 0)
    col = lax.broadcasted_iota(jnp.int32, (c, c), 1)
    blk = jnp.bitwise_xor(row, col)
    eye = jnp.where(row == col, 1.0, 0.0)
    bs = min(INV_BASE, c)
    ps = [jnp.where(blk < bs, -l, 0.0) for l in l_mats]
    ts = [eye + p for p in ps]
    n = 2
    while n < bs:
        ps = [_dot(p, p) for p in ps]
        tps = [_dot(t, p) for t, p in zip(ts, ps)]
        ts = [t + tp for t, tp in zip(ts, tps)]
        n *= 2
    while bs < c:
        es = [jnp.where((blk < 2 * bs) & (blk >= bs), l, 0.0) for l in l_mats]
        tes = [_dot(t, e) for t, e in zip(ts, es)]
        tets = [_dot(te, t) for te, t in zip(tes, ts)]
        ts = [t - tet for t, tet in zip(ts, tets)]
        bs *= 2
    lts = _dot3_each(list(zip(l_mats, ts)))
    trs = [_dot(t, eye - t - lt) for t, lt in zip(ts, lts)]
    return [t + tr for t, tr in zip(ts, trs)]


def _dn_seq_kernel(stack_ref, hm_ref, beta_ref, g_ref, s0_ref, nw_ref,
                   o_ref, sn_ref,
                   u_ref, wq_ref, attn_ref, kdt_ref, egl_ref, *, c, bb, n_kh):
    step = pl.program_id(1)
    dk = HEAD_DIM
    hv = 2 * n_kh
    n_conv = 4 * n_kh
    rows_per_step = hm_ref.shape[1]
    n_chunks = rows_per_step // c

    @pl.when(step == 0)
    def _():
        sn_ref[...] = s0_ref[...]

    row = lax.broadcasted_iota(jnp.int32, (c, c), 0)
    col = lax.broadcasted_iota(jnp.int32, (c, c), 1)
    same = (jnp.bitwise_xor(row, col) & (bb - 1)) == 0
    low_eq = same & (row >= col)
    low = same & (row > col)
    up_eq = same & (row <= col)
    last = same & (col >= c - bb)
    eye = row == col
    head_lane = lax.broadcasted_iota(jnp.int32, (1, hv), 1)
    member = lax.broadcasted_iota(jnp.int32, (c, 1), 0) & (bb - 1)


    def intra(units):
        qs = [hm_ref[hk, ci * c:(ci + 1) * c, :] for hk, ci in units]
        ks = [hm_ref[n_kh + hk, ci * c:(ci + 1) * c, :] for hk, ci in units]
        kks = [_dot_nt(k, k) for k in ks]
        qks = [_dot_nt(q, k) for q, k in zip(qs, ks)]
        probs = []
        for (hk, ci), q, k, kk, qk in zip(units, qs, ks, kks, qks):
            r0 = ci * c
            beta_blk = beta_ref[r0:r0 + c, :]
            g_blk = g_ref[r0:r0 + c, :]
            for j in range(2):
                h = 2 * hk + j
                sel = head_lane == h
                beta = jnp.sum(jnp.where(sel, beta_blk, 0.0), axis=1, keepdims=True)
                g = jnp.sum(jnp.where(sel, g_blk, 0.0), axis=1, keepdims=True)
                g_row = jnp.sum(jnp.where(eye, g, 0.0), axis=0, keepdims=True)
                gc = jnp.sum(jnp.where(low_eq, g_row, 0.0), axis=1, keepdims=True)
                gc_row = jnp.sum(jnp.where(up_eq, g, 0.0), axis=0, keepdims=True)
                g_last = jnp.sum(jnp.where(last, gc_row, 0.0), axis=1, keepdims=True)
                decay = jnp.exp(jnp.where(low_eq, gc - gc_row, -jnp.inf))
                egc = jnp.exp(gc)
                attn_ref[h, r0:r0 + c, :] = (qk * decay).astype(BF16)
                wq_ref[h, ci, c:2 * c, :] = (q * egc).astype(BF16)
                kdt_ref[h, ci] = (k * jnp.exp(g_last - gc)).T.astype(BF16)
                egl_ref[h, ci] = jnp.broadcast_to(egc[c - bb:c, :], (bb, dk))
                v = hm_ref[2 * n_kh + h, r0:r0 + c, :]
                rhs = jnp.concatenate([v * beta, k * (beta * egc)], axis=1)
                probs.append((h, ci, jnp.where(low, kk * beta * decay, 0.0), rhs))
        t_mats = _unit_lower_inverse_each([p[2] for p in probs], c)
        uws = [_dot(t, p[3]) for t, p in zip(t_mats, probs)]
        for (h, ci, _, _), uw in zip(probs, uws):
            u_ref[h, ci * c:(ci + 1) * c, :] = uw[:, :dk]
            wq_ref[h, ci, 0:c, :] = uw[:, dk:].astype(BF16)

    kh_per_iter = min(n_kh, max(1, INTRA_UNITS // n_chunks))

    def intra_body(it, carry):
        intra([(it * kh_per_iter + m, ci) for m in range(kh_per_iter) for ci in range(n_chunks)])
        return carry

    lax.fori_loop(0, n_kh // kh_per_iter, intra_body, 0)

    def inter(ci, heads):
        rows = pl.ds(pl.multiple_of(ci * c, c), c)
        mm = lambda x, y: jnp.dot(x, y, preferred_element_type=F32)
        res = [[mm(wq_ref[h, ci], sn_ref[s, h].astype(BF16)) for s in range(bb)] for h in heads]
        v_news, o_states = [], []
        for h, res_h in zip(heads, res):
            v_new = u_ref[h, rows, :]
            o_state = None
            for s, r in enumerate(res_h):
                if bb == 1:
                    v_new = v_new - r[:c]
                    o_state = r[c:]
                else:
                    mine = member == s
                    v_new = v_new - jnp.where(mine, r[:c], 0.0)
                    o_state = jnp.where(mine, r[c:], 0.0 if o_state is None else o_state)
            v_news.append(v_new)
            o_states.append(o_state)
        v_seq = [[(v if bb == 1 else jnp.where(member == s, v, 0.0)).astype(BF16) for s in range(bb)]
                 for v in v_news]
        attn_v = [mm(attn_ref[h, rows, :], v.astype(BF16)) for h, v in zip(heads, v_news)]
        upd = [[mm(kdt_ref[h, ci], v_s) for v_s in v_h] for h, v_h in zip(heads, v_seq)]
        for h, o_state, av, upd_h in zip(heads, o_states, attn_v, upd):
            for s in range(bb):
                sn_ref[s, h] = sn_ref[s, h] * egl_ref[h, ci, s:s + 1, :] + upd_h[s]
            z = hm_ref[n_conv + h, rows, :]
            o_ref[h, rows, :] = (_rms(o_state + av, nw_ref[...]) * _silu(z)).astype(BF16)

    heads_per_iter = min(hv, max(1, INTER_PROBLEMS // bb))

    def inter_body(ci, carry):
        if heads_per_iter == hv:
            inter(ci, list(range(hv)))
        else:
            def head_body(it, carry2):
                inter(ci, [it * heads_per_iter + m for m in range(heads_per_iter)])
                return carry2
            lax.fori_loop(0, hv // heads_per_iter, head_body, 0)
        return carry

    lax.fori_loop(0, n_chunks, inter_body, 0)


def _dn_seq(hm, beta, g, s0_all, layer, stack, norm_w, c, bb, rows_per_step):
    n_blk, t, dk = hm.shape
    n_grp = s0_all.shape[1] // bb
    hv = s0_all.shape[2]
    n_kh = hv // 2
    n_conv = 4 * n_kh
    rows_per_grp = t // n_grp
    steps = rows_per_grp // rows_per_step
    n_chunks = rows_per_step // c
    assert n_blk == n_conv + hv and rows_per_step % c == 0 and rows_per_grp % rows_per_step == 0
    assert bb & (bb - 1) == 0 and c & (c - 1) == 0 and c % bb == 0
    kern = functools.partial(_dn_seq_kernel, c=c, bb=bb, n_kh=n_kh)
    tok = lambda i, j: i * steps + j
    state = pl.BlockSpec((None, bb, hv, dk, dk), lambda i, j: (layer, i, 0, 0, 0))
    gate = pl.BlockSpec((rows_per_step, hv), lambda i, j: (tok(i, j), 0))
    return pl.pallas_call(
        kern,
        grid=(n_grp, steps),
        in_specs=[pl.BlockSpec(memory_space=pl.ANY),
                  pl.BlockSpec((n_blk, rows_per_step, dk), lambda i, j: (0, tok(i, j), 0)),
                  gate, gate, state, norm_w.spec],
        out_specs=[pl.BlockSpec((hv, rows_per_step, dk), lambda i, j: (0, tok(i, j), 0)), state],
        out_shape=[jax.ShapeDtypeStruct((hv, t, dk), BF16), jax.ShapeDtypeStruct(s0_all.shape, F32)],
        scratch_shapes=[pltpu.VMEM((hv, rows_per_step, dk), F32),
                        pltpu.VMEM((hv, n_chunks, 2 * c, dk), BF16),
                        pltpu.VMEM((hv, rows_per_step, c), BF16),
                        pltpu.VMEM((hv, n_chunks, dk, c), BF16),
                        pltpu.VMEM((hv, n_chunks, bb, dk), F32)],
        input_output_aliases={} if stack is None else {0: 1},
        compiler_params=_params("parallel", "arbitrary"),
        name="dn_seq",
    )(s0_all if stack is None else stack, hm, beta, g, s0_all, norm_w.arr)


def _dn_out_kernel(o_ref, x_ref, ng_ref, w_ref, y_ref, *, post):
    o = jnp.concatenate([o_ref[h] for h in range(o_ref.shape[0])], axis=1)
    y = _dot(o, w_ref[...])
    y_ref[...] = x_ref[...] + _rms(y, ng_ref[post:post + 1, :])


def _dn_out(o, x, ng, w_out, post):
    t, d = x.shape
    hv, _, dk = o.shape
    tm = min(TOKEN_TILE, t)
    kern = functools.partial(_dn_out_kernel, post=post)
    return pl.pallas_call(
        kern,
        grid=(t // tm,),
        in_specs=[pl.BlockSpec((hv, tm, dk), lambda i: (0, i, 0)),
                  pl.BlockSpec((tm, d), lambda i: (i, 0)), ng.spec, w_out.spec],
        out_specs=pl.BlockSpec((tm, d), lambda i: (i, 0)),
        out_shape=jax.ShapeDtypeStruct((t, d), F32),
        compiler_params=_params("parallel"),
        name="dn_out",
    )(o, x, ng.arr, w_out.arr)


def _dn_layer(x, conv_past, s0_all, layer, stack, ng, w_in, a_log, dt_bias, conv_w, norm_w, w_out):
    b, s, d = x.shape
    dk = HEAD_DIM
    conv_dim = conv_past.shape[-1]
    n_conv = conv_dim // dk
    assert s >= CONV_PAST
    if s >= CHUNK:
        bb, c, rows_per_step = 1, CHUNK, min(s, 4 * CHUNK)
    else:
        bb = min(SHORT_GROUP, b)
        c = rows_per_step = s * bb
    n_grp = b // bb
    xt = x.reshape(n_grp, bb, s, d).transpose(0, 2, 1, 3).reshape(b * s, d)
    cp = conv_past.reshape(n_grp, bb, CONV_PAST, n_conv, dk).transpose(0, 3, 2, 1, 4)
    cp = cp.reshape(n_grp, n_conv, CONV_PAST * bb, dk)
    hm, beta, g, tail = _dn_proj(xt, ng, w_in, a_log, dt_bias, cp, conv_w, bb, pre=2)
    o, stack = _dn_seq(hm, beta, g, s0_all, layer, stack, norm_w, c, bb, rows_per_step)
    y = _dn_out(o, xt, ng, w_out, post=3)
    y = y.reshape(n_grp, s, bb, d).transpose(0, 2, 1, 3).reshape(b, s, d)
    new_conv = tail.reshape(n_grp, n_conv, CONV_PAST, bb, dk).transpose(0, 3, 2, 1, 4)
    return y, new_conv.reshape(b, CONV_PAST, conv_dim), stack


def _trunk(x, st_pool, st_conv, st_delta, pos0, p):
    b, s, d = x.shape
    depth = p["wg"].shape[0]
    new_pool, new_conv, new_delta = [], [], None
    for i in range(depth):
        ng = _Res(p["ng"], (i,))
        x = _ffn(x.reshape(b * s, d), ng, p["wg"], p["wu"], p["wd"], i, 0).reshape(b, s, d)
        j = i // 2
        if i % 2 == 0:
            pool_w, pool_sc = _Res(p["pool_w"], (j,)), _Res(p["pool_scale"], (j,))
            if s >= 2 * POOL_PAST:
                x, ps = _pool_seq(x, st_pool[j], ng, pool_w, pool_sc, pos0, 2, 3)
            else:
                xt, pt = _pool_step(x.transpose(1, 0, 2), st_pool[j].transpose(1, 0, 2), ng,
                                    pool_w, pool_sc, pos0, 2, 3)
                x, ps = xt.transpose(1, 0, 2), pt.transpose(1, 0, 2)
            new_pool.append(ps)
        else:
            x, cs, new_delta = _dn_layer(
                x, st_conv[j], st_delta, j, new_delta, ng, _Res(p["w_in"], (j,)),
                _Res(p["a_log"], (j,)), _Res(p["dt_bias"], (j,)), _Res(p["conv_w"], (j,)),
                _Res(p["norm_w"], (j,)), _Res(p["w_out"], (j,)))
            new_conv.append(cs)
        x = _ffn(x.reshape(b * s, d), ng, p["wg"], p["wu"], p["wd"], i, 1).reshape(b, s, d)
    return x, jnp.stack(new_pool), jnp.stack(new_conv), new_delta


def kernel(x_prompt, x_sample, state_pool, state_conv, state_delta, norm_gains, w_ffn_gate,
           w_ffn_up, w_ffn_down, pool_w, pool_scale, dn_w_in, dn_conv_w, dn_a_log, dn_dt_bias,
           dn_norm_w, dn_w_out):
    n_dn, _, _, conv_dim = state_conv.shape
    hv = state_delta.shape[2]
    n_pool = state_pool.shape[0]
    d = x_prompt.shape[-1]
    n_conv = conv_dim // HEAD_DIM
    p = {
        "ng": norm_gains,
        "wg": w_ffn_gate, "wu": w_ffn_up, "wd": w_ffn_down,
        "pool_w": pool_w.astype(BF16), "pool_scale": pool_scale.reshape(n_pool, 1, d),
        "w_in": dn_w_in.astype(BF16),
        "a_log": dn_a_log.reshape(n_dn, 1, hv), "dt_bias": dn_dt_bias.reshape(n_dn, 1, hv),
        "conv_w": dn_conv_w.reshape(n_dn, CONV_W, n_conv, HEAD_DIM).transpose(0, 2, 1, 3),
        "norm_w": dn_norm_w.reshape(n_dn, 1, HEAD_DIM),
        "w_out": dn_w_out.astype(BF16),
    }
    bp = x_prompt.shape[0]
    zp = jnp.zeros((n_pool, bp) + state_pool.shape[2:], F32)
    zc = jnp.zeros((n_dn, bp) + state_conv.shape[2:], F32)
    zd = jnp.zeros((n_dn, bp) + state_delta.shape[2:], F32)
    y_p, pool_p, conv_p, delta_p = _trunk(x_prompt, zp, zc, zd, 0, p)
    y_s, pool_s, conv_s, delta_s = _trunk(x_sample, state_pool, state_conv, state_delta, PAST_LEN, p)
    return (y_p, y_s, pool_p, conv_p, delta_p, pool_s, conv_s, delta_s)
```

```python
import functools
from typing import NamedTuple

import jax
import jax.numpy as jnp
from jax import lax
from jax.experimental import pallas as pl
from jax.experimental.pallas import tpu as pltpu

F32 = jnp.float32
BF16 = jnp.bfloat16

EPS = 1e-6
PAST_LEN = 16384
POOL_WINDOWS = (2, 4, 8, 16)
POOL_PAST = max(POOL_WINDOWS) - 1
HEAD_DIM = 128
CONV_W = 4
CONV_PAST = CONV_W - 1
CHUNK = 64
SHORT_GROUP = 8
INV_BASE = 16
INTRA_UNITS = 16
INTER_PROBLEMS = 32
CONV_PIECE = 128

VMEM_LIMIT = 56 * 1024 * 1024
TOKEN_TILE = 512
FFN_SUBTILES = 2
FFN_CHUNK = 256
FFN_STAGE_SLOTS = 4


def _params(*sem):
    return pltpu.CompilerParams(dimension_semantics=sem, vmem_limit_bytes=VMEM_LIMIT)


class _Res(NamedTuple):
    arr: jax.Array
    lead: tuple = ()

    @property
    def spec(self):
        n = len(self.lead)
        index = tuple(self.lead) + (0,) * (self.arr.ndim - n)
        return pl.BlockSpec((None,) * n + tuple(self.arr.shape[n:]), lambda *_: index,
                            pipeline_mode=pl.Buffered(1))


def _rms(x, g):
    return x * lax.rsqrt(jnp.mean(x * x, axis=-1, keepdims=True) + EPS) * g


def _silu(x):
    return x / (1.0 + jnp.exp(-x))


def _dot(a, b):
    return jnp.dot(a.astype(BF16), b.astype(BF16), preferred_element_type=F32)


def _dot_nt(a, b):
    return lax.dot_general(a.astype(BF16), b.astype(BF16), (((1,), (1,)), ((), ())),
                           preferred_element_type=F32)


def _ffn_compute(x_ref, ng_ref, wg_ref, wu_ref, wd_ref, o_ref, h_ref, acc_ref, *, pre, post, n_sub,
                 before_chunk=lambda c: None):
    tm = x_ref.shape[0] // n_sub
    rows = [slice(i * tm, (i + 1) * tm) for i in range(n_sub)]
    for r in rows:
        h_ref[r, :] = _rms(x_ref[r, :], ng_ref[pre:pre + 1, :]).astype(BF16)
    for c in range(wg_ref.shape[0]):
        before_chunk(c)
        for r in rows:
            gate = jnp.dot(h_ref[r, :], wg_ref[c], preferred_element_type=F32)
            up = jnp.dot(h_ref[r, :], wu_ref[c], preferred_element_type=F32)
            part = _dot(_silu(gate) * up, wd_ref[c])
            acc_ref[r, :] = part if c == 0 else acc_ref[r, :] + part
    for r in rows:
        o_ref[r, :] = x_ref[r, :] + 0.5 * _rms(acc_ref[r, :], ng_ref[post:post + 1, :])


def _ffn_kernel(x_ref, ng_ref, wg_ref, wu_ref, wd_ref, o_ref, h_ref, acc_ref, *, pre, post, n_sub):
    _ffn_compute(x_ref, ng_ref, wg_ref, wu_ref, wd_ref, o_ref, h_ref, acc_ref, pre=pre, post=post, n_sub=n_sub)


def _ffn(x, ng, wg, wu, wd, half):
    t, d = x.shape
    n_sub = FFN_SUBTILES if t % (FFN_SUBTILES * TOKEN_TILE) == 0 else 1
    tm = min(n_sub * TOKEN_TILE, t)
    kern = functools.partial(_ffn_kernel, pre=4 * half, post=4 * half + 1, n_sub=n_sub)
    return pl.pallas_call(
        kern,
        grid=(t // tm,),
        in_specs=[pl.BlockSpec((tm, d), lambda i: (i, 0)), ng.spec, wg.spec, wu.spec, wd.spec],
        out_specs=pl.BlockSpec((tm, d), lambda i: (i, 0)),
        out_shape=jax.ShapeDtypeStruct((t, d), F32),
        scratch_shapes=[pltpu.VMEM((tm, d), BF16), pltpu.VMEM((tm, d), F32)],
        compiler_params=_params("parallel"),
        name="ffn",
    )(x, ng.arr, wg.arr, wu.arr, wd.arr)


def _ffn_cast_kernel(x_ref, ng_ref, wg_hbm, wu_hbm, wd_hbm, o_ref, wgb_hbm, wub_hbm, wdb_hbm,
                     wg_ref, wu_ref, wd_ref, sg_ref, su_ref, sd_ref, h_ref, acc_ref, sem_in, sem_out,
                     *, layer, half, pre, post):
    n_chunks, _, tf = wg_ref.shape
    n_slots = sg_ref.shape[0]

    def loads(c):
        slot = c % n_slots
        cols = pl.ds(c * tf, tf)
        return (pltpu.make_async_copy(wg_hbm.at[layer, half, :, cols], sg_ref.at[slot], sem_in.at[0, slot]),
                pltpu.make_async_copy(wu_hbm.at[layer, half, :, cols], su_ref.at[slot], sem_in.at[1, slot]),
                pltpu.make_async_copy(wd_hbm.at[layer, half, cols, :], sd_ref.at[slot], sem_in.at[2, slot]))

    def stores(c):
        return (pltpu.make_async_copy(wg_ref.at[c], wgb_hbm.at[c], sem_out.at[0, c]),
                pltpu.make_async_copy(wu_ref.at[c], wub_hbm.at[c], sem_out.at[1, c]),
                pltpu.make_async_copy(wd_ref.at[c], wdb_hbm.at[c], sem_out.at[2, c]))

    def fetch(c):
        slot = c % n_slots
        for cp in loads(c):
            cp.wait()
        wg_ref[c] = sg_ref[slot].astype(BF16)
        wu_ref[c] = su_ref[slot].astype(BF16)
        wd_ref[c] = sd_ref[slot].astype(BF16)
        for cp in stores(c):
            cp.start()
        if c + n_slots < n_chunks:
            for cp in loads(c + n_slots):
                cp.start()

    for c in range(min(n_slots, n_chunks)):
        for cp in loads(c):
            cp.start()
    _ffn_compute(x_ref, ng_ref, wg_ref, wu_ref, wd_ref, o_ref, h_ref, acc_ref, pre=pre, post=post,
                 n_sub=1, before_chunk=fetch)
    for c in range(n_chunks):
        for cp in stores(c):
            cp.wait()


def _ffn_cast(x, ng, wg, wu, wd, layer, half):
    t, d = x.shape
    d_ff = wg.shape[-1]
    tf = FFN_CHUNK
    n_chunks = d_ff // tf
    assert t <= 2 * TOKEN_TILE and d_ff % tf == 0
    kern = functools.partial(_ffn_cast_kernel, layer=layer, half=half, pre=4 * half, post=4 * half + 1)
    hbm = pl.BlockSpec(memory_space=pl.ANY)
    col_chunks = jax.ShapeDtypeStruct((n_chunks, d, tf), BF16)
    row_chunks = jax.ShapeDtypeStruct((n_chunks, tf, d), BF16)
    return pl.pallas_call(
        kern,
        grid=(1,),
        in_specs=[pl.BlockSpec((t, d), lambda i: (0, 0)), ng.spec, hbm, hbm, hbm],
        out_specs=[pl.BlockSpec((t, d), lambda i: (0, 0)), hbm, hbm, hbm],
        out_shape=[jax.ShapeDtypeStruct((t, d), F32), col_chunks, col_chunks, row_chunks],
        scratch_shapes=[pltpu.VMEM(col_chunks.shape, BF16), pltpu.VMEM(col_chunks.shape, BF16),
                        pltpu.VMEM(row_chunks.shape, BF16),
                        pltpu.VMEM((FFN_STAGE_SLOTS, d, tf), F32), pltpu.VMEM((FFN_STAGE_SLOTS, d, tf), F32),
                        pltpu.VMEM((FFN_STAGE_SLOTS, tf, d), F32),
                        pltpu.VMEM((t, d), BF16), pltpu.VMEM((t, d), F32),
                        pltpu.SemaphoreType.DMA((3, FFN_STAGE_SLOTS)),
                        pltpu.SemaphoreType.DMA((3, n_chunks))],
        compiler_params=_params("arbitrary"),
        name="ffn_cast",
    )(x, ng.arr, wg, wu, wd)


def _pool_seq_kernel(x_ref, past_ref, ng_ref, w_ref, sc_ref, o_ref, np_ref, hp_ref,
                     *, ts, pos0, pre, post):
    s = pl.program_id(1)
    off = POOL_PAST + 1
    d = x_ref.shape[-1]
    gw = d // len(POOL_WINDOWS)

    @pl.when(s == 0)
    def _():
        hp_ref[0:1, :] = jnp.zeros((1, d), F32)
        hp_ref[1:off, :] = past_ref[0]

    x = x_ref[0]
    h = _rms(x, ng_ref[pre:pre + 1, :])
    hp_ref[off:off + ts, :] = h
    pos = pos0 + s * ts + lax.broadcasted_iota(jnp.int32, (ts, 1), 0)
    outs = []
    for gi, w in enumerate(POOL_WINDOWS):
        lo, hi = gi * gw, (gi + 1) * gw
        hg = h[:, lo:hi]
        win = hg
        for k in range(1, w):
            win = win + hp_ref[off - k:off - k + ts, lo:hi]
        cnt = jnp.minimum(pos + 1, w).astype(F32)
        outs.append(_dot(win / cnt - hg, w_ref[gi]))
    y = jnp.concatenate(outs, axis=-1) * sc_ref[...]
    o_ref[0] = x + _rms(y, ng_ref[post:post + 1, :])
    tail = hp_ref[ts + 1:ts + off, :]
    np_ref[0] = tail
    hp_ref[1:off, :] = tail


def _pool_seq(x, past, ng, w, sc, pos0, pre, post):
    b, s, d = x.shape
    ts = min(TOKEN_TILE, s)
    kern = functools.partial(_pool_seq_kernel, ts=ts, pos0=pos0, pre=pre, post=post)
    return pl.pallas_call(
        kern,
        grid=(b, s // ts),
        in_specs=[pl.BlockSpec((1, ts, d), lambda i, j: (i, j, 0)),
                  pl.BlockSpec((1, POOL_PAST, d), lambda i, j: (i, 0, 0)),
                  ng.spec, w.spec, sc.spec],
        out_specs=[pl.BlockSpec((1, ts, d), lambda i, j: (i, j, 0)),
                   pl.BlockSpec((1, POOL_PAST, d), lambda i, j: (i, 0, 0))],
        out_shape=[jax.ShapeDtypeStruct((b, s, d), F32),
                   jax.ShapeDtypeStruct((b, POOL_PAST, d), F32)],
        scratch_shapes=[pltpu.VMEM((POOL_PAST + 1 + ts, d), F32)],
        compiler_params=_params("parallel", "arbitrary"),
        name="pool_seq",
    )(x, past, ng.arr, w.arr, sc.arr)


def _pool_step_kernel(x_ref, past_ref, ng_ref, w_ref, sc_ref, o_ref, np_ref, *, pos0, pre, post):
    steps, bb, d = x_ref.shape
    gw = d // len(POOL_WINDOWS)
    xs = [x_ref[t] for t in range(steps)]
    hs = [_rms(x, ng_ref[pre:pre + 1, :]) for x in xs]
    hp = [past_ref[p] for p in range(POOL_PAST)] + hs
    outs = []
    for gi, w in enumerate(POOL_WINDOWS):
        lo, hi = gi * gw, (gi + 1) * gw
        rows = []
        for t in range(steps):
            win = hp[POOL_PAST + t][:, lo:hi]
            for k in range(1, w):
                win = win + hp[POOL_PAST + t - k][:, lo:hi]
            cnt = float(min(pos0 + t + 1, w))
            rows.append(win / cnt - hs[t][:, lo:hi])
        outs.append(_dot(jnp.concatenate(rows, axis=0), w_ref[gi]))
    y = jnp.concatenate(outs, axis=-1) * sc_ref[...]
    for t in range(steps):
        o_ref[t] = xs[t] + _rms(y[t * bb:(t + 1) * bb], ng_ref[post:post + 1, :])
    for p in range(POOL_PAST):
        np_ref[p] = hp[steps + p]


def _pool_step(x, past, ng, w, sc, pos0, pre, post):
    steps, b, d = x.shape
    bb = min(32, b)
    kern = functools.partial(_pool_step_kernel, pos0=pos0, pre=pre, post=post)
    return pl.pallas_call(
        kern,
        grid=(b // bb,),
        in_specs=[pl.BlockSpec((steps, bb, d), lambda i: (0, i, 0)),
                  pl.BlockSpec((POOL_PAST, bb, d), lambda i: (0, i, 0)),
                  ng.spec, w.spec, sc.spec],
        out_specs=[pl.BlockSpec((steps, bb, d), lambda i: (0, i, 0)),
                   pl.BlockSpec((POOL_PAST, bb, d), lambda i: (0, i, 0))],
        out_shape=[jax.ShapeDtypeStruct((steps, b, d), F32),
                   jax.ShapeDtypeStruct((POOL_PAST, b, d), F32)],
        compiler_params=_params("parallel"),
        name="pool_step",
    )(x, past, ng.arr, w.arr, sc.arr)


def _dn_proj_kernel(x_ref, ng_ref, w_ref, alog_ref, dtb_ref, cp_ref, cw_ref,
                    hm_ref, beta_ref, g_ref, tail_ref, buf_ref, carry_ref,
                    *, pre, tn, n_kh, bb, tiles_per_grp):
    i = pl.program_id(0)
    tm = x_ref.shape[0]
    n_blk = hm_ref.shape[0]
    hv = 2 * n_kh
    n_conv = 4 * n_kh
    n_seg = cp_ref.shape[0]
    seg = tm // n_seg
    halo = CONV_PAST * bb
    off = buf_ref.shape[1] - seg
    per = tn // HEAD_DIM
    carried = tiles_per_grp > 1
    h = _rms(x_ref[...], ng_ref[pre:pre + 1, :]).astype(BF16)

    if carried:
        @pl.when(i % tiles_per_grp == 0)
        def _():
            carry_ref[...] = cp_ref[0]

    def slot_of(c, m, s):
        return ((c % 2) * per + m) * n_seg + s

    def stage(c, res):
        for m in range(per):
            cb = c * per + m
            blk = res[:, m * HEAD_DIM:(m + 1) * HEAD_DIM]
            if cb >= n_conv:
                hm_ref[cb] = blk
                continue
            for s in range(n_seg):
                slot = slot_of(c, m, s)
                buf_ref[slot, off - halo:off, :] = carry_ref[cb] if carried else cp_ref[s, cb]
                buf_ref[slot, off:off + seg, :] = blk[s * seg:(s + 1) * seg, :]
                tail = buf_ref[slot, off + seg - halo:off + seg, :]
                if carried:
                    carry_ref[cb] = tail
                tail_ref[s, cb] = tail

    def convolve(c):
        piece = min(seg, CONV_PIECE)
        for m in range(per):
            cb = c * per + m
            if cb >= n_conv:
                continue
            for s in range(n_seg):
                slot = slot_of(c, m, s)
                for r0 in range(0, seg, piece):
                    acc = None
                    for j in range(CONV_W):
                        lo = off - halo + j * bb + r0
                        term = buf_ref[slot, lo:lo + piece, :] * cw_ref[cb, j:j + 1, :]
                        acc = term if acc is None else acc + term
                    out = _silu(acc)
                    if cb < 2 * n_kh:
                        out = out * lax.rsqrt(jnp.sum(out * out, axis=-1, keepdims=True) + EPS)
                        if cb < n_kh:
                            out = out * (HEAD_DIM ** -0.5)
                    hm_ref[cb, s * seg + r0:s * seg + r0 + piece, :] = out

    n_chunks = n_blk // per
    for c in range(n_chunks):
        res = jnp.dot(h, w_ref[:, c * tn:(c + 1) * tn], preferred_element_type=F32)
        if c > 0:
            convolve(c - 1)
        stage(c, res)
    convolve(n_chunks - 1)
    gates = jnp.dot(h, w_ref[:, n_blk * HEAD_DIM:], preferred_element_type=F32)
    b = gates[:, :hv]
    a = gates[:, hv:] + dtb_ref[...]
    beta_ref[...] = 1.0 / (1.0 + jnp.exp(-b))
    softplus = jnp.maximum(a, 0.0) + jnp.log1p(jnp.exp(-jnp.abs(a)))
    g_ref[...] = -jnp.exp(alog_ref[...]) * softplus


def _dn_proj(x, ng, w_in, a_log, dt_bias, conv_past, conv_w, bb, pre):
    t, d = x.shape
    tm = min(TOKEN_TILE, t)
    hv = a_log.arr.shape[-1]
    n_blk = (w_in.arr.shape[-1] - 2 * hv) // HEAD_DIM
    n_grp, n_conv, halo, dk = conv_past.shape
    rows_per_grp = t // n_grp
    n_seg = max(1, tm // rows_per_grp)
    tiles_per_grp = max(1, rows_per_grp // tm)
    assert (rows_per_grp % tm == 0 or tm % rows_per_grp == 0) and tm // n_seg >= halo
    tn = 4 * HEAD_DIM
    kern = functools.partial(_dn_proj_kernel, pre=pre, tn=tn, n_kh=hv // 2, bb=bb,
                             tiles_per_grp=tiles_per_grp)
    row = lambda n: pl.BlockSpec((tm, n), lambda i: (i, 0))
    past = pl.BlockSpec((n_seg, n_conv, halo, dk), lambda i: (i // tiles_per_grp, 0, 0, 0))
    off = -(-halo // 8) * 8
    return pl.pallas_call(
        kern,
        grid=(t // tm,),
        in_specs=[row(d), ng.spec, w_in.spec, a_log.spec, dt_bias.spec, past, conv_w.spec],
        out_specs=[pl.BlockSpec((n_blk, tm, HEAD_DIM), lambda i: (0, i, 0)), row(hv), row(hv), past],
        out_shape=[jax.ShapeDtypeStruct((n_blk, t, HEAD_DIM), F32),
                   jax.ShapeDtypeStruct((t, hv), F32), jax.ShapeDtypeStruct((t, hv), F32),
                   jax.ShapeDtypeStruct(conv_past.shape, F32)],
        scratch_shapes=[pltpu.VMEM((2 * (tn // HEAD_DIM) * n_seg, off + tm // n_seg, dk), F32),
                        pltpu.VMEM((n_conv, halo, dk), F32)],
        compiler_params=_params("arbitrary"),
        name="dn_proj",
    )(x, ng.arr, w_in.arr, a_log.arr, dt_bias.arr, conv_past, conv_w.arr)


def _dot3_each(pairs):
    split = lambda x: (x.astype(BF16), (x - x.astype(BF16).astype(F32)).astype(BF16))
    parts = [(split(a), split(b)) for a, b in pairs]
    mm = lambda x, y: jnp.dot(x, y, preferred_element_type=F32)
    hh = [mm(a[0], b[0]) for a, b in parts]
    hl = [mm(a[0], b[1]) for a, b in parts]
    lh = [mm(a[1], b[0]) for a, b in parts]
    return [x + (y + z) for x, y, z in zip(hh, hl, lh)]


def _unit_lower_inverse_each(l_mats, c):
    row = lax.broadcasted_iota(jnp.int32, (c, c), 0)
    col = lax.broadcasted_iota(jnp.int32, (c, c), 1)
    blk = jnp.bitwise_xor(row, col)
    eye = jnp.where(row == col, 1.0, 0.0)
    bs = min(INV_BASE, c)
    ps = [jnp.where(blk < bs, -l, 0.0) for l in l_mats]
    ts = [eye + p for p in ps]
    n = 2
    while n < bs:
        ps = [_dot(p, p) for p in ps]
        tps = [_dot(t, p) for t, p in zip(ts, ps)]
        ts = [t + tp for t, tp in zip(ts, tps)]
        n *= 2
    while bs < c:
        es = [jnp.where((blk < 2 * bs) & (blk >= bs), l, 0.0) for l in l_mats]
        tes = [_dot(t, e) for t, e in zip(ts, es)]
        tets = [_dot(te, t) for te, t in zip(tes, ts)]
        ts = [t - tet for t, tet in zip(ts, tets)]
        bs *= 2
    lts = _dot3_each(list(zip(l_mats, ts)))
    trs = [_dot(t, eye - t - lt) for t, lt in zip(ts, lts)]
    return [t + tr for t, tr in zip(ts, trs)]


def _dn_seq_kernel(stack_ref, hm_ref, beta_ref, g_ref, s0_ref, nw_ref,
                   o_ref, sn_ref,
                   u_ref, wq_ref, attn_ref, kdt_ref, egl_ref, *, c, bb, n_kh):
    step = pl.program_id(1)
    dk = HEAD_DIM
    hv = 2 * n_kh
    n_conv = 4 * n_kh
    rows_per_step = hm_ref.shape[1]
    n_chunks = rows_per_step // c

    @pl.when(step == 0)
    def _():
        sn_ref[...] = s0_ref[...]

    row = lax.broadcasted_iota(jnp.int32, (c, c), 0)
    col = lax.broadcasted_iota(jnp.int32, (c, c), 1)
    same = (jnp.bitwise_xor(row, col) & (bb - 1)) == 0
    low_eq = same & (row >= col)
    low = same & (row > col)
    up_eq = same & (row <= col)
    last = same & (col >= c - bb)
    eye = row == col
    head_lane = lax.broadcasted_iota(jnp.int32, (1, hv), 1)
    member = lax.broadcasted_iota(jnp.int32, (c, 1), 0) & (bb - 1)


    def intra(units):
        qs = [hm_ref[hk, ci * c:(ci + 1) * c, :] for hk, ci in units]
        ks = [hm_ref[n_kh + hk, ci * c:(ci + 1) * c, :] for hk, ci in units]
        kks = [_dot_nt(k, k) for k in ks]
        qks = [_dot_nt(q, k) for q, k in zip(qs, ks)]
        probs = []
        for (hk, ci), q, k, kk, qk in zip(units, qs, ks, kks, qks):
            r0 = ci * c
            beta_blk = beta_ref[r0:r0 + c, :]
            g_blk = g_ref[r0:r0 + c, :]
            for j in range(2):
                h = 2 * hk + j
                sel = head_lane == h
                beta = jnp.sum(jnp.where(sel, beta_blk, 0.0), axis=1, keepdims=True)
                g = jnp.sum(jnp.where(sel, g_blk, 0.0), axis=1, keepdims=True)
                g_row = jnp.sum(jnp.where(eye, g, 0.0), axis=0, keepdims=True)
                gc = jnp.sum(jnp.where(low_eq, g_row, 0.0), axis=1, keepdims=True)
                gc_row = jnp.sum(jnp.where(up_eq, g, 0.0), axis=0, keepdims=True)
                g_last = jnp.sum(jnp.where(last, gc_row, 0.0), axis=1, keepdims=True)
                decay = jnp.exp(jnp.where(low_eq, gc - gc_row, -jnp.inf))
                egc = jnp.exp(gc)
                attn_ref[h, r0:r0 + c, :] = (qk * decay).astype(BF16)
                wq_ref[h, ci, c:2 * c, :] = (q * egc).astype(BF16)
                kdt_ref[h, ci] = (k * jnp.exp(g_last - gc)).T.astype(BF16)
                egl_ref[h, ci] = jnp.broadcast_to(egc[c - bb:c, :], (bb, dk))
                v = hm_ref[2 * n_kh + h, r0:r0 + c, :]
                rhs = jnp.concatenate([v * beta, k * (beta * egc)], axis=1)
                probs.append((h, ci, jnp.where(low, kk * beta * decay, 0.0), rhs))
        t_mats = _unit_lower_inverse_each([p[2] for p in probs], c)
        uws = [_dot(t, p[3]) for t, p in zip(t_mats, probs)]
        for (h, ci, _, _), uw in zip(probs, uws):
            u_ref[h, ci * c:(ci + 1) * c, :] = uw[:, :dk]
            wq_ref[h, ci, 0:c, :] = uw[:, dk:].astype(BF16)

    kh_per_iter = min(n_kh, max(1, INTRA_UNITS // n_chunks))

    def intra_body(it, carry):
        intra([(it * kh_per_iter + m, ci) for m in range(kh_per_iter) for ci in range(n_chunks)])
        return carry

    lax.fori_loop(0, n_kh // kh_per_iter, intra_body, 0)

    def inter(ci, heads):
        rows = pl.ds(pl.multiple_of(ci * c, c), c)
        mm = lambda x, y: jnp.dot(x, y, preferred_element_type=F32)
        res = [[mm(wq_ref[h, ci], sn_ref[s, h].astype(BF16)) for s in range(bb)] for h in heads]
        v_news, o_states = [], []
        for h, res_h in zip(heads, res):
            v_new = u_ref[h, rows, :]
            o_state = None
            for s, r in enumerate(res_h):
                if bb == 1:
                    v_new = v_new - r[:c]
                    o_state = r[c:]
                else:
                    mine = member == s
                    v_new = v_new - jnp.where(mine, r[:c], 0.0)
                    o_state = jnp.where(mine, r[c:], 0.0 if o_state is None else o_state)
            v_news.append(v_new)
            o_states.append(o_state)
        v_seq = [[(v if bb == 1 else jnp.where(member == s, v, 0.0)).astype(BF16) for s in range(bb)]
                 for v in v_news]
        attn_v = [mm(attn_ref[h, rows, :], v.astype(BF16)) for h, v in zip(heads, v_news)]
        upd = [[mm(kdt_ref[h, ci], v_s) for v_s in v_h] for h, v_h in zip(heads, v_seq)]
        for h, o_state, av, upd_h in zip(heads, o_states, attn_v, upd):
            for s in range(bb):
                sn_ref[s, h] = sn_ref[s, h] * egl_ref[h, ci, s:s + 1, :] + upd_h[s]
            z = hm_ref[n_conv + h, rows, :]
            o_ref[h, rows, :] = (_rms(o_state + av, nw_ref[...]) * _silu(z)).astype(BF16)

    heads_per_iter = min(hv, max(1, INTER_PROBLEMS // bb))

    def inter_body(ci, carry):
        if heads_per_iter == hv:
            inter(ci, list(range(hv)))
        else:
            def head_body(it, carry2):
                inter(ci, [it * heads_per_iter + m for m in range(heads_per_iter)])
                return carry2
            lax.fori_loop(0, hv // heads_per_iter, head_body, 0)
        return carry

    lax.fori_loop(0, n_chunks, inter_body, 0)


def _dn_seq(hm, beta, g, s0_all, layer, stack, norm_w, c, bb, rows_per_step):
    n_blk, t, dk = hm.shape
    n_grp = s0_all.shape[1] // bb
    hv = s0_all.shape[2]
    n_kh = hv // 2
    n_conv = 4 * n_kh
    rows_per_grp = t // n_grp
    steps = rows_per_grp // rows_per_step
    n_chunks = rows_per_step // c
    assert n_blk == n_conv + hv and rows_per_step % c == 0 and rows_per_grp % rows_per_step == 0
    assert bb & (bb - 1) == 0 and c & (c - 1) == 0 and c % bb == 0
    kern = functools.partial(_dn_seq_kernel, c=c, bb=bb, n_kh=n_kh)
    tok = lambda i, j: i * steps + j
    state = pl.BlockSpec((None, bb, hv, dk, dk), lambda i, j: (layer, i, 0, 0, 0))
    gate = pl.BlockSpec((rows_per_step, hv), lambda i, j: (tok(i, j), 0))
    return pl.pallas_call(
        kern,
        grid=(n_grp, steps),
        in_specs=[pl.BlockSpec(memory_space=pl.ANY),
                  pl.BlockSpec((n_blk, rows_per_step, dk), lambda i, j: (0, tok(i, j), 0)),
                  gate, gate, state, norm_w.spec],
        out_specs=[pl.BlockSpec((hv, rows_per_step, dk), lambda i, j: (0, tok(i, j), 0)), state],
        out_shape=[jax.ShapeDtypeStruct((hv, t, dk), BF16), jax.ShapeDtypeStruct(s0_all.shape, F32)],
        scratch_shapes=[pltpu.VMEM((hv, rows_per_step, dk), F32),
                        pltpu.VMEM((hv, n_chunks, 2 * c, dk), BF16),
                        pltpu.VMEM((hv, rows_per_step, c), BF16),
                        pltpu.VMEM((hv, n_chunks, dk, c), BF16),
                        pltpu.VMEM((hv, n_chunks, bb, dk), F32)],
        input_output_aliases={} if stack is None else {0: 1},
        compiler_params=_params("parallel", "arbitrary"),
        name="dn_seq",
    )(s0_all if stack is None else stack, hm, beta, g, s0_all, norm_w.arr)


def _dn_out_kernel(o_ref, x_ref, ng_ref, w_ref, y_ref, *, post):
    o = jnp.concatenate([o_ref[h] for h in range(o_ref.shape[0])], axis=1)
    y = _dot(o, w_ref[...])
    y_ref[...] = x_ref[...] + _rms(y, ng_ref[post:post + 1, :])


def _dn_out(o, x, ng, w_out, post):
    t, d = x.shape
    hv, _, dk = o.shape
    tm = min(TOKEN_TILE, t)
    kern = functools.partial(_dn_out_kernel, post=post)
    return pl.pallas_call(
        kern,
        grid=(t // tm,),
        in_specs=[pl.BlockSpec((hv, tm, dk), lambda i: (0, i, 0)),
                  pl.BlockSpec((tm, d), lambda i: (i, 0)), ng.spec, w_out.spec],
        out_specs=pl.BlockSpec((tm, d), lambda i: (i, 0)),
        out_shape=jax.ShapeDtypeStruct((t, d), F32),
        compiler_params=_params("parallel"),
        name="dn_out",
    )(o, x, ng.arr, w_out.arr)


def _dn_layer(x, conv_past, s0_all, layer, stack, ng, w_in, a_log, dt_bias, conv_w, norm_w, w_out):
    b, s, d = x.shape
    dk = HEAD_DIM
    conv_dim = conv_past.shape[-1]
    n_conv = conv_dim // dk
    assert s >= CONV_PAST
    if s >= CHUNK:
        bb, c, rows_per_step = 1, CHUNK, min(s, 4 * CHUNK)
    else:
        bb = min(SHORT_GROUP, b)
        c = rows_per_step = s * bb
    n_grp = b // bb
    xt = x.reshape(n_grp, bb, s, d).transpose(0, 2, 1, 3).reshape(b * s, d)
    cp = conv_past.reshape(n_grp, bb, CONV_PAST, n_conv, dk).transpose(0, 3, 2, 1, 4)
    cp = cp.reshape(n_grp, n_conv, CONV_PAST * bb, dk)
    hm, beta, g, tail = _dn_proj(xt, ng, w_in, a_log, dt_bias, cp, conv_w, bb, pre=2)
    o, stack = _dn_seq(hm, beta, g, s0_all, layer, stack, norm_w, c, bb, rows_per_step)
    y = _dn_out(o, xt, ng, w_out, post=3)
    y = y.reshape(n_grp, s, bb, d).transpose(0, 2, 1, 3).reshape(b, s, d)
    new_conv = tail.reshape(n_grp, n_conv, CONV_PAST, bb, dk).transpose(0, 3, 2, 1, 4)
    return y, new_conv.reshape(b, CONV_PAST, conv_dim), stack


def _trunk(x, st_pool, st_conv, st_delta, pos0, p, ffn):
    b, s, d = x.shape
    depth = p["ng"].shape[0]
    new_pool, new_conv, new_delta = [], [], None
    for i in range(depth):
        ng = _Res(p["ng"], (i,))
        x = ffn(x.reshape(b * s, d), ng, i, 0).reshape(b, s, d)
        j = i // 2
        if i % 2 == 0:
            pool_w, pool_sc = _Res(p["pool_w"], (j,)), _Res(p["pool_scale"], (j,))
            if s >= 2 * POOL_PAST:
                x, ps = _pool_seq(x, st_pool[j], ng, pool_w, pool_sc, pos0, 2, 3)
            else:
                xt, pt = _pool_step(x.transpose(1, 0, 2), st_pool[j].transpose(1, 0, 2), ng,
                                    pool_w, pool_sc, pos0, 2, 3)
                x, ps = xt.transpose(1, 0, 2), pt.transpose(1, 0, 2)
            new_pool.append(ps)
        else:
            x, cs, new_delta = _dn_layer(
                x, st_conv[j], st_delta, j, new_delta, ng, _Res(p["w_in"], (j,)),
                _Res(p["a_log"], (j,)), _Res(p["dt_bias"], (j,)), _Res(p["conv_w"], (j,)),
                _Res(p["norm_w"], (j,)), _Res(p["w_out"], (j,)))
            new_conv.append(cs)
        x = ffn(x.reshape(b * s, d), ng, i, 1).reshape(b, s, d)
    return x, jnp.stack(new_pool), jnp.stack(new_conv), new_delta


def kernel(x_prompt, x_sample, state_pool, state_conv, state_delta, norm_gains, w_ffn_gate,
           w_ffn_up, w_ffn_down, pool_w, pool_scale, dn_w_in, dn_conv_w, dn_a_log, dn_dt_bias,
           dn_norm_w, dn_w_out):
    n_dn, _, _, conv_dim = state_conv.shape
    hv = state_delta.shape[2]
    n_pool = state_pool.shape[0]
    d = x_prompt.shape[-1]
    n_conv = conv_dim // HEAD_DIM
    p = {
        "ng": norm_gains,
        "pool_w": pool_w.astype(BF16), "pool_scale": pool_scale.reshape(n_pool, 1, d),
        "w_in": dn_w_in.astype(BF16),
        "a_log": dn_a_log.reshape(n_dn, 1, hv), "dt_bias": dn_dt_bias.reshape(n_dn, 1, hv),
        "conv_w": dn_conv_w.reshape(n_dn, CONV_W, n_conv, HEAD_DIM).transpose(0, 2, 1, 3),
        "norm_w": dn_norm_w.reshape(n_dn, 1, HEAD_DIM),
        "w_out": dn_w_out.astype(BF16),
    }
    bp = x_prompt.shape[0]
    zp = jnp.zeros((n_pool, bp) + state_pool.shape[2:], F32)
    zc = jnp.zeros((n_dn, bp) + state_conv.shape[2:], F32)
    zd = jnp.zeros((n_dn, bp) + state_delta.shape[2:], F32)
    bf16_w = {}

    def ffn_sample(x2, ng, layer, half):
        y, *bf16_w[layer, half] = _ffn_cast(x2, ng, w_ffn_gate, w_ffn_up, w_ffn_down, layer, half)
        return y

    def ffn_prompt(x2, ng, layer, half):
        return _ffn(x2, ng, *(_Res(w) for w in bf16_w[layer, half]), half)

    y_s, pool_s, conv_s, delta_s = _trunk(x_sample, state_pool, state_conv, state_delta, PAST_LEN, p, ffn_sample)
    y_p, pool_p, conv_p, delta_p = _trunk(x_prompt, zp, zc, zd, 0, p, ffn_prompt)
    return (y_p, y_s, pool_p, conv_p, delta_p, pool_s, conv_s, delta_s)
```

```python
import functools
from typing import NamedTuple

import jax
import jax.numpy as jnp
from jax import lax
from jax.experimental import pallas as pl
from jax.experimental.pallas import tpu as pltpu

F32 = jnp.float32
BF16 = jnp.bfloat16

EPS = 1e-6
PAST_LEN = 16384
POOL_WINDOWS = (2, 4, 8, 16)
POOL_PAST = max(POOL_WINDOWS) - 1
HEAD_DIM = 128
CONV_W = 4
CONV_PAST = CONV_W - 1
CHUNK = 64
SHORT_GROUP = 8
INV_BASE = 16
INTRA_UNITS = 16
INTER_PROBLEMS = 32
CONV_PIECE = 128

VMEM_LIMIT = 56 * 1024 * 1024
TOKEN_TILE = 512
FFN_SUBTILES = 2
FFN_CHUNK = 256
FFN_STAGE_SLOTS = 4


def _params(*sem):
    return pltpu.CompilerParams(dimension_semantics=sem, vmem_limit_bytes=VMEM_LIMIT)


class _Res(NamedTuple):
    arr: jax.Array
    lead: tuple = ()

    @property
    def spec(self):
        n = len(self.lead)
        index = tuple(self.lead) + (0,) * (self.arr.ndim - n)
        return pl.BlockSpec((None,) * n + tuple(self.arr.shape[n:]), lambda *_: index,
                            pipeline_mode=pl.Buffered(1))


def _rms(x, g):
    return x * lax.rsqrt(jnp.mean(x * x, axis=-1, keepdims=True) + EPS) * g


def _silu(x):
    return x / (1.0 + jnp.exp(-x))


def _dot(a, b):
    return jnp.dot(a.astype(BF16), b.astype(BF16), preferred_element_type=F32)


def _dot_nt(a, b):
    return lax.dot_general(a.astype(BF16), b.astype(BF16), (((1,), (1,)), ((), ())),
                           preferred_element_type=F32)


def _ffn_compute(x_ref, ng_ref, wg_ref, wu_ref, wd_ref, o_ref, h_ref, acc_ref, *, pre, post, n_sub,
                 before_chunk=lambda c: None):
    tm = x_ref.shape[0] // n_sub
    rows = [slice(i * tm, (i + 1) * tm) for i in range(n_sub)]
    for r in rows:
        h_ref[r, :] = _rms(x_ref[r, :], ng_ref[pre:pre + 1, :]).astype(BF16)
    for c in range(wg_ref.shape[0]):
        before_chunk(c)
        for r in rows:
            gate = jnp.dot(h_ref[r, :], wg_ref[c], preferred_element_type=F32)
            up = jnp.dot(h_ref[r, :], wu_ref[c], preferred_element_type=F32)
            part = _dot(_silu(gate) * up, wd_ref[c])
            acc_ref[r, :] = part if c == 0 else acc_ref[r, :] + part
    for r in rows:
        o_ref[r, :] = x_ref[r, :] + 0.5 * _rms(acc_ref[r, :], ng_ref[post:post + 1, :])


def _ffn_kernel(x_ref, ng_ref, wg_ref, wu_ref, wd_ref, o_ref, h_ref, acc_ref, *, pre, post, n_sub):
    _ffn_compute(x_ref, ng_ref, wg_ref, wu_ref, wd_ref, o_ref, h_ref, acc_ref, pre=pre, post=post, n_sub=n_sub)


def _ffn(x, ng, wg, wu, wd, half):
    t, d = x.shape
    n_sub = FFN_SUBTILES if t % (FFN_SUBTILES * TOKEN_TILE) == 0 else 1
    tm = min(n_sub * TOKEN_TILE, t)
    kern = functools.partial(_ffn_kernel, pre=4 * half, post=4 * half + 1, n_sub=n_sub)
    return pl.pallas_call(
        kern,
        grid=(t // tm,),
        in_specs=[pl.BlockSpec((tm, d), lambda i: (i, 0)), ng.spec, wg.spec, wu.spec, wd.spec],
        out_specs=pl.BlockSpec((tm, d), lambda i: (i, 0)),
        out_shape=jax.ShapeDtypeStruct((t, d), F32),
        scratch_shapes=[pltpu.VMEM((tm, d), BF16), pltpu.VMEM((tm, d), F32)],
        compiler_params=_params("parallel"),
        name="ffn",
    )(x, ng.arr, wg.arr, wu.arr, wd.arr)


def _ffn_cast_kernel(x_ref, ng_ref, wg_hbm, wu_hbm, wd_hbm, o_ref, wgb_hbm, wub_hbm, wdb_hbm,
                     wg_ref, wu_ref, wd_ref, sg_ref, su_ref, sd_ref, h_ref, acc_ref, sem_in, sem_out,
                     *, layer, half, pre, post):
    n_chunks, _, tf = wg_ref.shape
    n_slots = sg_ref.shape[0]

    def loads(c):
        slot = c % n_slots
        cols = pl.ds(c * tf, tf)
        return (pltpu.make_async_copy(wg_hbm.at[layer, half, :, cols], sg_ref.at[slot], sem_in.at[0, slot]),
                pltpu.make_async_copy(wu_hbm.at[layer, half, :, cols], su_ref.at[slot], sem_in.at[1, slot]),
                pltpu.make_async_copy(wd_hbm.at[layer, half, cols, :], sd_ref.at[slot], sem_in.at[2, slot]))

    def stores(c):
        return (pltpu.make_async_copy(wg_ref.at[c], wgb_hbm.at[c], sem_out.at[0, c]),
                pltpu.make_async_copy(wu_ref.at[c], wub_hbm.at[c], sem_out.at[1, c]),
                pltpu.make_async_copy(wd_ref.at[c], wdb_hbm.at[c], sem_out.at[2, c]))

    def fetch(c):
        slot = c % n_slots
        for cp in loads(c):
            cp.wait()
        wg_ref[c] = sg_ref[slot].astype(BF16)
        wu_ref[c] = su_ref[slot].astype(BF16)
        wd_ref[c] = sd_ref[slot].astype(BF16)
        for cp in stores(c):
            cp.start()
        if c + n_slots < n_chunks:
            for cp in loads(c + n_slots):
                cp.start()

    for c in range(min(n_slots, n_chunks)):
        for cp in loads(c):
            cp.start()
    _ffn_compute(x_ref, ng_ref, wg_ref, wu_ref, wd_ref, o_ref, h_ref, acc_ref, pre=pre, post=post,
                 n_sub=1, before_chunk=fetch)
    for c in range(n_chunks):
        for cp in stores(c):
            cp.wait()


def _ffn_cast(x, ng, wg, wu, wd, layer, half):
    t, d = x.shape
    d_ff = wg.shape[-1]
    tf = FFN_CHUNK
    n_chunks = d_ff // tf
    assert t <= 2 * TOKEN_TILE and d_ff % tf == 0
    kern = functools.partial(_ffn_cast_kernel, layer=layer, half=half, pre=4 * half, post=4 * half + 1)
    hbm = pl.BlockSpec(memory_space=pl.ANY)
    col_chunks = jax.ShapeDtypeStruct((n_chunks, d, tf), BF16)
    row_chunks = jax.ShapeDtypeStruct((n_chunks, tf, d), BF16)
    return pl.pallas_call(
        kern,
        grid=(1,),
        in_specs=[pl.BlockSpec((t, d), lambda i: (0, 0)), ng.spec, hbm, hbm, hbm],
        out_specs=[pl.BlockSpec((t, d), lambda i: (0, 0)), hbm, hbm, hbm],
        out_shape=[jax.ShapeDtypeStruct((t, d), F32), col_chunks, col_chunks, row_chunks],
        scratch_shapes=[pltpu.VMEM(col_chunks.shape, BF16), pltpu.VMEM(col_chunks.shape, BF16),
                        pltpu.VMEM(row_chunks.shape, BF16),
                        pltpu.VMEM((FFN_STAGE_SLOTS, d, tf), F32), pltpu.VMEM((FFN_STAGE_SLOTS, d, tf), F32),
                        pltpu.VMEM((FFN_STAGE_SLOTS, tf, d), F32),
                        pltpu.VMEM((t, d), BF16), pltpu.VMEM((t, d), F32),
                        pltpu.SemaphoreType.DMA((3, FFN_STAGE_SLOTS)),
                        pltpu.SemaphoreType.DMA((3, n_chunks))],
        compiler_params=_params("arbitrary"),
        name="ffn_cast",
    )(x, ng.arr, wg, wu, wd)


def _pool_seq_kernel(x_ref, past_ref, ng_ref, w_ref, sc_ref, o_ref, np_ref, hp_ref, lv_ref,
                     *, ts, pos0, pre, post):
    s = pl.program_id(1)
    base = POOL_PAST + 1
    off = 2 * base
    n = off + ts
    d = x_ref.shape[-1]
    gw = d // len(POOL_WINDOWS)

    @pl.when(s == 0)
    def _():
        hp_ref[0:off - POOL_PAST, :] = jnp.zeros((off - POOL_PAST, d), F32)
        hp_ref[off - POOL_PAST:off, :] = past_ref[0]
        lv_ref[:, 0:base, :] = jnp.zeros((lv_ref.shape[0], base, d), F32)

    x = x_ref[0]
    h = _rms(x, ng_ref[pre:pre + 1, :])
    hp_ref[off:n, :] = h
    pos = pos0 + s * ts + lax.broadcasted_iota(jnp.int32, (ts, 1), 0)
    outs = []
    src = hp_ref
    for gi, w in enumerate(POOL_WINDOWS):
        lo, hi = gi * gw, (gi + 1) * gw
        half = w // 2
        if gi + 1 < len(POOL_WINDOWS):
            dst = lv_ref.at[gi]
            dst[base:n, lo:] = src[base:n, lo:] + src[base - half:n - half, lo:]
            win = dst[off:n, lo:hi]
            src = dst
        else:
            win = src[off:n, lo:hi] + src[off - half:n - half, lo:hi]
        cnt = jnp.minimum(pos + 1, w).astype(F32)
        outs.append(_dot(win / cnt - h[:, lo:hi], w_ref[gi]))
    y = jnp.concatenate(outs, axis=-1) * sc_ref[...]
    o_ref[0] = x + _rms(y, ng_ref[post:post + 1, :])
    tail = hp_ref[n - POOL_PAST:n, :]
    np_ref[0] = tail
    hp_ref[off - POOL_PAST:off, :] = tail


def _pool_seq(x, past, ng, w, sc, pos0, pre, post):
    b, s, d = x.shape
    ts = min(TOKEN_TILE, s)
    assert all(w == 2 << i for i, w in enumerate(POOL_WINDOWS))
    kern = functools.partial(_pool_seq_kernel, ts=ts, pos0=pos0, pre=pre, post=post)
    return pl.pallas_call(
        kern,
        grid=(b, s // ts),
        in_specs=[pl.BlockSpec((1, ts, d), lambda i, j: (i, j, 0)),
                  pl.BlockSpec((1, POOL_PAST, d), lambda i, j: (i, 0, 0)),
                  ng.spec, w.spec, sc.spec],
        out_specs=[pl.BlockSpec((1, ts, d), lambda i, j: (i, j, 0)),
                   pl.BlockSpec((1, POOL_PAST, d), lambda i, j: (i, 0, 0))],
        out_shape=[jax.ShapeDtypeStruct((b, s, d), F32),
                   jax.ShapeDtypeStruct((b, POOL_PAST, d), F32)],
        scratch_shapes=[pltpu.VMEM((2 * (POOL_PAST + 1) + ts, d), F32),
                        pltpu.VMEM((len(POOL_WINDOWS) - 1, 2 * (POOL_PAST + 1) + ts, d), F32)],
        compiler_params=_params("parallel", "arbitrary"),
        name="pool_seq",
    )(x, past, ng.arr, w.arr, sc.arr)


def _pool_step_kernel(x_ref, past_ref, ng_ref, w_ref, sc_ref, o_ref, np_ref, *, pos0, pre, post):
    steps, bb, d = x_ref.shape
    gw = d // len(POOL_WINDOWS)
    xs = [x_ref[t] for t in range(steps)]
    hs = [_rms(x, ng_ref[pre:pre + 1, :]) for x in xs]
    hp = [past_ref[p] for p in range(POOL_PAST)] + hs
    outs = []
    for gi, w in enumerate(POOL_WINDOWS):
        lo, hi = gi * gw, (gi + 1) * gw
        rows = []
        for t in range(steps):
            win = hp[POOL_PAST + t][:, lo:hi]
            for k in range(1, w):
                win = win + hp[POOL_PAST + t - k][:, lo:hi]
            cnt = float(min(pos0 + t + 1, w))
            rows.append(win / cnt - hs[t][:, lo:hi])
        outs.append(_dot(jnp.concatenate(rows, axis=0), w_ref[gi]))
    y = jnp.concatenate(outs, axis=-1) * sc_ref[...]
    for t in range(steps):
        o_ref[t] = xs[t] + _rms(y[t * bb:(t + 1) * bb], ng_ref[post:post + 1, :])
    for p in range(POOL_PAST):
        np_ref[p] = hp[steps + p]


def _pool_step(x, past, ng, w, sc, pos0, pre, post):
    steps, b, d = x.shape
    bb = min(32, b)
    kern = functools.partial(_pool_step_kernel, pos0=pos0, pre=pre, post=post)
    return pl.pallas_call(
        kern,
        grid=(b // bb,),
        in_specs=[pl.BlockSpec((steps, bb, d), lambda i: (0, i, 0)),
                  pl.BlockSpec((POOL_PAST, bb, d), lambda i: (0, i, 0)),
                  ng.spec, w.spec, sc.spec],
        out_specs=[pl.BlockSpec((steps, bb, d), lambda i: (0, i, 0)),
                   pl.BlockSpec((POOL_PAST, bb, d), lambda i: (0, i, 0))],
        out_shape=[jax.ShapeDtypeStruct((steps, b, d), F32),
                   jax.ShapeDtypeStruct((POOL_PAST, b, d), F32)],
        compiler_params=_params("parallel"),
        name="pool_step",
    )(x, past, ng.arr, w.arr, sc.arr)


def _dn_proj_kernel(x_ref, ng_ref, w_ref, alog_ref, dtb_ref, cp_ref, cw_ref,
                    hm_ref, beta_ref, g_ref, tail_ref, buf_ref, carry_ref,
                    *, pre, tn, n_kh, bb, tiles_per_grp):
    i = pl.program_id(0)
    tm = x_ref.shape[0]
    n_blk = hm_ref.shape[0]
    hv = 2 * n_kh
    n_conv = 4 * n_kh
    n_seg = cp_ref.shape[0]
    seg = tm // n_seg
    halo = CONV_PAST * bb
    off = buf_ref.shape[1] - seg
    per = tn // HEAD_DIM
    carried = tiles_per_grp > 1
    h = _rms(x_ref[...], ng_ref[pre:pre + 1, :]).astype(BF16)

    if carried:
        @pl.when(i % tiles_per_grp == 0)
        def _():
            carry_ref[...] = cp_ref[0]

    def slot_of(c, m, s):
        return ((c % 2) * per + m) * n_seg + s

    def stage(c, res):
        for m in range(per):
            cb = c * per + m
            blk = res[:, m * HEAD_DIM:(m + 1) * HEAD_DIM]
            if cb >= n_conv:
                hm_ref[cb] = blk
                continue
            for s in range(n_seg):
                slot = slot_of(c, m, s)
                buf_ref[slot, off - halo:off, :] = carry_ref[cb] if carried else cp_ref[s, cb]
                buf_ref[slot, off:off + seg, :] = blk[s * seg:(s + 1) * seg, :]
                tail = buf_ref[slot, off + seg - halo:off + seg, :]
                if carried:
                    carry_ref[cb] = tail
                tail_ref[s, cb] = tail

    def convolve(c):
        piece = min(seg, CONV_PIECE)
        for m in range(per):
            cb = c * per + m
            if cb >= n_conv:
                continue
            for s in range(n_seg):
                slot = slot_of(c, m, s)
                for r0 in range(0, seg, piece):
                    acc = None
                    for j in range(CONV_W):
                        lo = off - halo + j * bb + r0
                        term = buf_ref[slot, lo:lo + piece, :] * cw_ref[cb, j:j + 1, :]
                        acc = term if acc is None else acc + term
                    out = _silu(acc)
                    if cb < 2 * n_kh:
                        out = out * lax.rsqrt(jnp.sum(out * out, axis=-1, keepdims=True) + EPS)
                        if cb < n_kh:
                            out = out * (HEAD_DIM ** -0.5)
                    hm_ref[cb, s * seg + r0:s * seg + r0 + piece, :] = out

    n_chunks = n_blk // per
    for c in range(n_chunks):
        res = jnp.dot(h, w_ref[:, c * tn:(c + 1) * tn], preferred_element_type=F32)
        if c > 0:
            convolve(c - 1)
        stage(c, res)
    convolve(n_chunks - 1)
    gates = jnp.dot(h, w_ref[:, n_blk * HEAD_DIM:], preferred_element_type=F32)
    b = gates[:, :hv]
    a = gates[:, hv:] + dtb_ref[...]
    beta_ref[...] = 1.0 / (1.0 + jnp.exp(-b))
    softplus = jnp.maximum(a, 0.0) + jnp.log1p(jnp.exp(-jnp.abs(a)))
    g_ref[...] = -jnp.exp(alog_ref[...]) * softplus


def _dn_proj(x, ng, w_in, a_log, dt_bias, conv_past, conv_w, bb, pre):
    t, d = x.shape
    tm = min(TOKEN_TILE, t)
    hv = a_log.arr.shape[-1]
    n_blk = (w_in.arr.shape[-1] - 2 * hv) // HEAD_DIM
    n_grp, n_conv, halo, dk = conv_past.shape
    rows_per_grp = t // n_grp
    n_seg = max(1, tm // rows_per_grp)
    tiles_per_grp = max(1, rows_per_grp // tm)
    assert (rows_per_grp % tm == 0 or tm % rows_per_grp == 0) and tm // n_seg >= halo
    tn = 4 * HEAD_DIM
    kern = functools.partial(_dn_proj_kernel, pre=pre, tn=tn, n_kh=hv // 2, bb=bb,
                             tiles_per_grp=tiles_per_grp)
    row = lambda n: pl.BlockSpec((tm, n), lambda i: (i, 0))
    past = pl.BlockSpec((n_seg, n_conv, halo, dk), lambda i: (i // tiles_per_grp, 0, 0, 0))
    off = -(-halo // 8) * 8
    return pl.pallas_call(
        kern,
        grid=(t // tm,),
        in_specs=[row(d), ng.spec, w_in.spec, a_log.spec, dt_bias.spec, past, conv_w.spec],
        out_specs=[pl.BlockSpec((n_blk, tm, HEAD_DIM), lambda i: (0, i, 0)), row(hv), row(hv), past],
        out_shape=[jax.ShapeDtypeStruct((n_blk, t, HEAD_DIM), F32),
                   jax.ShapeDtypeStruct((t, hv), F32), jax.ShapeDtypeStruct((t, hv), F32),
                   jax.ShapeDtypeStruct(conv_past.shape, F32)],
        scratch_shapes=[pltpu.VMEM((2 * (tn // HEAD_DIM) * n_seg, off + tm // n_seg, dk), F32),
                        pltpu.VMEM((n_conv, halo, dk), F32)],
        compiler_params=_params("arbitrary"),
        name="dn_proj",
    )(x, ng.arr, w_in.arr, a_log.arr, dt_bias.arr, conv_past, conv_w.arr)


def _dot3_each(pairs):
    split = lambda x: (x.astype(BF16), (x - x.astype(BF16).astype(F32)).astype(BF16))
    parts = [(split(a), split(b)) for a, b in pairs]
    mm = lambda x, y: jnp.dot(x, y, preferred_element_type=F32)
    hh = [mm(a[0], b[0]) for a, b in parts]
    hl = [mm(a[0], b[1]) for a, b in parts]
    lh = [mm(a[1], b[0]) for a, b in parts]
    return [x + (y + z) for x, y, z in zip(hh, hl, lh)]


def _unit_lower_inverse_each(l_mats, c):
    row = lax.broadcasted_iota(jnp.int32, (c, c), 0)
    col = lax.broadcasted_iota(jnp.int32, (c, c), 1)
    blk = jnp.bitwise_xor(row, col)
    eye = jnp.where(row == col, 1.0, 0.0)
    bs = min(INV_BASE, c)
    ps = [jnp.where(blk < bs, -l, 0.0) for l in l_mats]
    ts = [eye + p for p in ps]
    n = 2
    while n < bs:
        ps = [_dot(p, p) for p in ps]
        tps = [_dot(t, p) for t, p in zip(ts, ps)]
        ts = [t + tp for t, tp in zip(ts, tps)]
        n *= 2
    while bs < c:
        es = [jnp.where((blk < 2 * bs) & (blk >= bs), l, 0.0) for l in l_mats]
        tes = [_dot(t, e) for t, e in zip(ts, es)]
        tets = [_dot(te, t) for te, t in zip(tes, ts)]
        ts = [t - tet for t, tet in zip(ts, tets)]
        bs *= 2
    lts = _dot3_each(list(zip(l_mats, ts)))
    trs = [_dot(t, eye - t - lt) for t, lt in zip(ts, lts)]
    return [t + tr for t, tr in zip(ts, trs)]


def _dn_seq_kernel(stack_ref, hm_ref, beta_ref, g_ref, s0_ref, nw_ref,
                   o_ref, sn_ref,
                   u_ref, wq_ref, attn_ref, kdt_ref, egl_ref, *, c, bb, n_kh):
    step = pl.program_id(1)
    dk = HEAD_DIM
    hv = 2 * n_kh
    n_conv = 4 * n_kh
    rows_per_step = hm_ref.shape[1]
    n_chunks = rows_per_step // c

    @pl.when(step == 0)
    def _():
        sn_ref[...] = s0_ref[...]

    row = lax.broadcasted_iota(jnp.int32, (c, c), 0)
    col = lax.broadcasted_iota(jnp.int32, (c, c), 1)
    same = (jnp.bitwise_xor(row, col) & (bb - 1)) == 0
    low_eq = same & (row >= col)
    low = same & (row > col)
    up_eq = same & (row <= col)
    last = same & (col >= c - bb)
    eye = row == col
    head_lane = lax.broadcasted_iota(jnp.int32, (1, hv), 1)
    member = lax.broadcasted_iota(jnp.int32, (c, 1), 0) & (bb - 1)


    def intra(units):
        qs = [hm_ref[hk, ci * c:(ci + 1) * c, :] for hk, ci in units]
        ks = [hm_ref[n_kh + hk, ci * c:(ci + 1) * c, :] for hk, ci in units]
        kks = [_dot_nt(k, k) for k in ks]
        qks = [_dot_nt(q, k) for q, k in zip(qs, ks)]
        probs = []
        for (hk, ci), q, k, kk, qk in zip(units, qs, ks, kks, qks):
            r0 = ci * c
            beta_blk = beta_ref[r0:r0 + c, :]
            g_blk = g_ref[r0:r0 + c, :]
            for j in range(2):
                h = 2 * hk + j
                sel = head_lane == h
                beta = jnp.sum(jnp.where(sel, beta_blk, 0.0), axis=1, keepdims=True)
                g = jnp.sum(jnp.where(sel, g_blk, 0.0), axis=1, keepdims=True)
                g_row = jnp.sum(jnp.where(eye, g, 0.0), axis=0, keepdims=True)
                gc = jnp.sum(jnp.where(low_eq, g_row, 0.0), axis=1, keepdims=True)
                gc_row = jnp.sum(jnp.where(up_eq, g, 0.0), axis=0, keepdims=True)
                g_last = jnp.sum(jnp.where(last, gc_row, 0.0), axis=1, keepdims=True)
                decay = jnp.exp(jnp.where(low_eq, gc - gc_row, -jnp.inf))
                egc = jnp.exp(gc)
                attn_ref[h, r0:r0 + c, :] = (qk * decay).astype(BF16)
                wq_ref[h, ci, c:2 * c, :] = (q * egc).astype(BF16)
                kdt_ref[h, ci] = (k * jnp.exp(g_last - gc)).T.astype(BF16)
                egl_ref[h, ci] = jnp.broadcast_to(egc[c - bb:c, :], (bb, dk))
                v = hm_ref[2 * n_kh + h, r0:r0 + c, :]
                rhs = jnp.concatenate([v * beta, k * (beta * egc)], axis=1)
                probs.append((h, ci, jnp.where(low, kk * beta * decay, 0.0), rhs))
        t_mats = _unit_lower_inverse_each([p[2] for p in probs], c)
        uws = [_dot(t, p[3]) for t, p in zip(t_mats, probs)]
        for (h, ci, _, _), uw in zip(probs, uws):
            u_ref[h, ci * c:(ci + 1) * c, :] = uw[:, :dk]
            wq_ref[h, ci, 0:c, :] = uw[:, dk:].astype(BF16)

    kh_per_iter = min(n_kh, max(1, INTRA_UNITS // n_chunks))

    def intra_body(it, carry):
        intra([(it * kh_per_iter + m, ci) for m in range(kh_per_iter) for ci in range(n_chunks)])
        return carry

    lax.fori_loop(0, n_kh // kh_per_iter, intra_body, 0)

    def inter(ci, heads):
        rows = pl.ds(pl.multiple_of(ci * c, c), c)
        mm = lambda x, y: jnp.dot(x, y, preferred_element_type=F32)
        res = [[mm(wq_ref[h, ci], sn_ref[s, h].astype(BF16)) for s in range(bb)] for h in heads]
        v_news, o_states = [], []
        for h, res_h in zip(heads, res):
            v_new = u_ref[h, rows, :]
            o_state = None
            for s, r in enumerate(res_h):
                if bb == 1:
                    v_new = v_new - r[:c]
                    o_state = r[c:]
                else:
                    mine = member == s
                    v_new = v_new - jnp.where(mine, r[:c], 0.0)
                    o_state = jnp.where(mine, r[c:], 0.0 if o_state is None else o_state)
            v_news.append(v_new)
            o_states.append(o_state)
        v_seq = [[(v if bb == 1 else jnp.where(member == s, v, 0.0)).astype(BF16) for s in range(bb)]
                 for v in v_news]
        attn_v = [mm(attn_ref[h, rows, :], v.astype(BF16)) for h, v in zip(heads, v_news)]
        upd = [[mm(kdt_ref[h, ci], v_s) for v_s in v_h] for h, v_h in zip(heads, v_seq)]
        for h, o_state, av, upd_h in zip(heads, o_states, attn_v, upd):
            for s in range(bb):
                sn_ref[s, h] = sn_ref[s, h] * egl_ref[h, ci, s:s + 1, :] + upd_h[s]
            z = hm_ref[n_conv + h, rows, :]
            o_ref[h, rows, :] = (_rms(o_state + av, nw_ref[...]) * _silu(z)).astype(BF16)

    heads_per_iter = min(hv, max(1, INTER_PROBLEMS // bb))

    def inter_body(ci, carry):
        if heads_per_iter == hv:
            inter(ci, list(range(hv)))
        else:
            def head_body(it, carry2):
                inter(ci, [it * heads_per_iter + m for m in range(heads_per_iter)])
                return carry2
            lax.fori_loop(0, hv // heads_per_iter, head_body, 0)
        return carry

    lax.fori_loop(0, n_chunks, inter_body, 0)


def _dn_seq(hm, beta, g, s0_all, layer, stack, norm_w, c, bb, rows_per_step):
    n_blk, t, dk = hm.shape
    n_grp = s0_all.shape[1] // bb
    hv = s0_all.shape[2]
    n_kh = hv // 2
    n_conv = 4 * n_kh
    rows_per_grp = t // n_grp
    steps = rows_per_grp // rows_per_step
    n_chunks = rows_per_step // c
    assert n_blk == n_conv + hv and rows_per_step % c == 0 and rows_per_grp % rows_per_step == 0
    assert bb & (bb - 1) == 0 and c & (c - 1) == 0 and c % bb == 0
    kern = functools.partial(_dn_seq_kernel, c=c, bb=bb, n_kh=n_kh)
    tok = lambda i, j: i * steps + j
    state = pl.BlockSpec((None, bb, hv, dk, dk), lambda i, j: (layer, i, 0, 0, 0))
    gate = pl.BlockSpec((rows_per_step, hv), lambda i, j: (tok(i, j), 0))
    return pl.pallas_call(
        kern,
        grid=(n_grp, steps),
        in_specs=[pl.BlockSpec(memory_space=pl.ANY),
                  pl.BlockSpec((n_blk, rows_per_step, dk), lambda i, j: (0, tok(i, j), 0)),
                  gate, gate, state, norm_w.spec],
        out_specs=[pl.BlockSpec((hv, rows_per_step, dk), lambda i, j: (0, tok(i, j), 0)), state],
        out_shape=[jax.ShapeDtypeStruct((hv, t, dk), BF16), jax.ShapeDtypeStruct(s0_all.shape, F32)],
        scratch_shapes=[pltpu.VMEM((hv, rows_per_step, dk), F32),
                        pltpu.VMEM((hv, n_chunks, 2 * c, dk), BF16),
                        pltpu.VMEM((hv, rows_per_step, c), BF16),
                        pltpu.VMEM((hv, n_chunks, dk, c), BF16),
                        pltpu.VMEM((hv, n_chunks, bb, dk), F32)],
        input_output_aliases={} if stack is None else {0: 1},
        compiler_params=_params("parallel", "arbitrary"),
        name="dn_seq",
    )(s0_all if stack is None else stack, hm, beta, g, s0_all, norm_w.arr)


def _dn_out_kernel(o_ref, x_ref, ng_ref, w_ref, y_ref, *, post):
    o = jnp.concatenate([o_ref[h] for h in range(o_ref.shape[0])], axis=1)
    y = _dot(o, w_ref[...])
    y_ref[...] = x_ref[...] + _rms(y, ng_ref[post:post + 1, :])


def _dn_out(o, x, ng, w_out, post):
    t, d = x.shape
    hv, _, dk = o.shape
    tm = min(TOKEN_TILE, t)
    kern = functools.partial(_dn_out_kernel, post=post)
    return pl.pallas_call(
        kern,
        grid=(t // tm,),
        in_specs=[pl.BlockSpec((hv, tm, dk), lambda i: (0, i, 0)),
                  pl.BlockSpec((tm, d), lambda i: (i, 0)), ng.spec, w_out.spec],
        out_specs=pl.BlockSpec((tm, d), lambda i: (i, 0)),
        out_shape=jax.ShapeDtypeStruct((t, d), F32),
        compiler_params=_params("parallel"),
        name="dn_out",
    )(o, x, ng.arr, w_out.arr)


def _dn_layer(x, conv_past, s0_all, layer, stack, ng, w_in, a_log, dt_bias, conv_w, norm_w, w_out):
    b, s, d = x.shape
    dk = HEAD_DIM
    conv_dim = conv_past.shape[-1]
    n_conv = conv_dim // dk
    assert s >= CONV_PAST
    if s >= CHUNK:
        bb, c, rows_per_step = 1, CHUNK, min(s, 4 * CHUNK)
    else:
        bb = min(SHORT_GROUP, b)
        c = rows_per_step = s * bb
    n_grp = b // bb
    xt = x.reshape(n_grp, bb, s, d).transpose(0, 2, 1, 3).reshape(b * s, d)
    cp = conv_past.reshape(n_grp, bb, CONV_PAST, n_conv, dk).transpose(0, 3, 2, 1, 4)
    cp = cp.reshape(n_grp, n_conv, CONV_PAST * bb, dk)
    hm, beta, g, tail = _dn_proj(xt, ng, w_in, a_log, dt_bias, cp, conv_w, bb, pre=2)
    o, stack = _dn_seq(hm, beta, g, s0_all, layer, stack, norm_w, c, bb, rows_per_step)
    y = _dn_out(o, xt, ng, w_out, post=3)
    y = y.reshape(n_grp, s, bb, d).transpose(0, 2, 1, 3).reshape(b, s, d)
    new_conv = tail.reshape(n_grp, n_conv, CONV_PAST, bb, dk).transpose(0, 3, 2, 1, 4)
    return y, new_conv.reshape(b, CONV_PAST, conv_dim), stack


def _trunk(x, st_pool, st_conv, st_delta, pos0, p, ffn):
    b, s, d = x.shape
    depth = p["ng"].shape[0]
    new_pool, new_conv, new_delta = [], [], None
    for i in range(depth):
        ng = _Res(p["ng"], (i,))
        x = ffn(x.reshape(b * s, d), ng, i, 0).reshape(b, s, d)
        j = i // 2
        if i % 2 == 0:
            pool_w, pool_sc = _Res(p["pool_w"], (j,)), _Res(p["pool_scale"], (j,))
            if s >= 2 * POOL_PAST:
                x, ps = _pool_seq(x, st_pool[j], ng, pool_w, pool_sc, pos0, 2, 3)
            else:
                xt, pt = _pool_step(x.transpose(1, 0, 2), st_pool[j].transpose(1, 0, 2), ng,
                                    pool_w, pool_sc, pos0, 2, 3)
                x, ps = xt.transpose(1, 0, 2), pt.transpose(1, 0, 2)
            new_pool.append(ps)
        else:
            x, cs, new_delta = _dn_layer(
                x, st_conv[j], st_delta, j, new_delta, ng, _Res(p["w_in"], (j,)),
                _Res(p["a_log"], (j,)), _Res(p["dt_bias"], (j,)), _Res(p["conv_w"], (j,)),
                _Res(p["norm_w"], (j,)), _Res(p["w_out"], (j,)))
            new_conv.append(cs)
        x = ffn(x.reshape(b * s, d), ng, i, 1).reshape(b, s, d)
    return x, jnp.stack(new_pool), jnp.stack(new_conv), new_delta


def kernel(x_prompt, x_sample, state_pool, state_conv, state_delta, norm_gains, w_ffn_gate,
           w_ffn_up, w_ffn_down, pool_w, pool_scale, dn_w_in, dn_conv_w, dn_a_log, dn_dt_bias,
           dn_norm_w, dn_w_out):
    n_dn, _, _, conv_dim = state_conv.shape
    hv = state_delta.shape[2]
    n_pool = state_pool.shape[0]
    d = x_prompt.shape[-1]
    n_conv = conv_dim // HEAD_DIM
    p = {
        "ng": norm_gains,
        "pool_w": pool_w.astype(BF16), "pool_scale": pool_scale.reshape(n_pool, 1, d),
        "w_in": dn_w_in.astype(BF16),
        "a_log": dn_a_log.reshape(n_dn, 1, hv), "dt_bias": dn_dt_bias.reshape(n_dn, 1, hv),
        "conv_w": dn_conv_w.reshape(n_dn, CONV_W, n_conv, HEAD_DIM).transpose(0, 2, 1, 3),
        "norm_w": dn_norm_w.reshape(n_dn, 1, HEAD_DIM),
        "w_out": dn_w_out.astype(BF16),
    }
    bp = x_prompt.shape[0]
    zp = jnp.zeros((n_pool, bp) + state_pool.shape[2:], F32)
    zc = jnp.zeros((n_dn, bp) + state_conv.shape[2:], F32)
    zd = jnp.zeros((n_dn, bp) + state_delta.shape[2:], F32)
    bf16_w = {}

    def ffn_sample(x2, ng, layer, half):
        y, *bf16_w[layer, half] = _ffn_cast(x2, ng, w_ffn_gate, w_ffn_up, w_ffn_down, layer, half)
        return y

    def ffn_prompt(x2, ng, layer, half):
        return _ffn(x2, ng, *(_Res(w) for w in bf16_w[layer, half]), half)

    y_s, pool_s, conv_s, delta_s = _trunk(x_sample, state_pool, state_conv, state_delta, PAST_LEN, p, ffn_sample)
    y_p, pool_p, conv_p, delta_p = _trunk(x_prompt, zp, zc, zd, 0, p, ffn_prompt)
    return (y_p, y_s, pool_p, conv_p, delta_p, pool_s, conv_s, delta_s)
```

```python
import functools
from typing import NamedTuple

import jax
import jax.numpy as jnp
from jax import lax
from jax.experimental import pallas as pl
from jax.experimental.pallas import tpu as pltpu

F32 = jnp.float32
BF16 = jnp.bfloat16

EPS = 1e-6
PAST_LEN = 16384
POOL_WINDOWS = (2, 4, 8, 16)
POOL_PAST = max(POOL_WINDOWS) - 1
HEAD_DIM = 128
CONV_W = 4
CONV_PAST = CONV_W - 1
CHUNK = 64
SHORT_GROUP = 8
INV_BASE = 16
INTRA_UNITS = 16
INTER_PROBLEMS = 32
CONV_PIECE = 128

VMEM_LIMIT = 56 * 1024 * 1024
TOKEN_TILE = 512
FFN_SUBTILES = 2
FFN_CHUNK = 256
FFN_STAGE_SLOTS = 4


def _params(*sem):
    return pltpu.CompilerParams(dimension_semantics=sem, vmem_limit_bytes=VMEM_LIMIT)


class _Res(NamedTuple):
    arr: jax.Array
    lead: tuple = ()

    @property
    def spec(self):
        n = len(self.lead)
        index = tuple(self.lead) + (0,) * (self.arr.ndim - n)
        return pl.BlockSpec((None,) * n + tuple(self.arr.shape[n:]), lambda *_: index,
                            pipeline_mode=pl.Buffered(1))


def _rms(x, g):
    return x * lax.rsqrt(jnp.mean(x * x, axis=-1, keepdims=True) + EPS) * g


def _silu(x):
    return x / (1.0 + jnp.exp(-x))


def _dot(a, b):
    return jnp.dot(a.astype(BF16), b.astype(BF16), preferred_element_type=F32)


def _dot_nt(a, b):
    return lax.dot_general(a.astype(BF16), b.astype(BF16), (((1,), (1,)), ((), ())),
                           preferred_element_type=F32)


def _ffn_compute(x_ref, ng_ref, wg_ref, wu_ref, wd_ref, o_ref, h_ref, acc_ref, *, pre, post, n_sub,
                 before_chunk=lambda c: None):
    tm = x_ref.shape[0] // n_sub
    rows = [slice(i * tm, (i + 1) * tm) for i in range(n_sub)]
    for r in rows:
        h_ref[r, :] = _rms(x_ref[r, :], ng_ref[pre:pre + 1, :]).astype(BF16)
    for c in range(wg_ref.shape[0]):
        before_chunk(c)
        for r in rows:
            gate = jnp.dot(h_ref[r, :], wg_ref[c], preferred_element_type=F32)
            up = jnp.dot(h_ref[r, :], wu_ref[c], preferred_element_type=F32)
            part = _dot(_silu(gate) * up, wd_ref[c])
            acc_ref[r, :] = part if c == 0 else acc_ref[r, :] + part
    for r in rows:
        o_ref[r, :] = x_ref[r, :] + 0.5 * _rms(acc_ref[r, :], ng_ref[post:post + 1, :])


def _ffn_kernel(x_ref, ng_ref, wg_ref, wu_ref, wd_ref, o_ref, h_ref, acc_ref, *, pre, post, n_sub):
    _ffn_compute(x_ref, ng_ref, wg_ref, wu_ref, wd_ref, o_ref, h_ref, acc_ref, pre=pre, post=post, n_sub=n_sub)


def _ffn(x, ng, wg, wu, wd, half):
    t, d = x.shape
    n_sub = FFN_SUBTILES if t % (FFN_SUBTILES * TOKEN_TILE) == 0 else 1
    tm = min(n_sub * TOKEN_TILE, t)
    kern = functools.partial(_ffn_kernel, pre=4 * half, post=4 * half + 1, n_sub=n_sub)
    return pl.pallas_call(
        kern,
        grid=(t // tm,),
        in_specs=[pl.BlockSpec((tm, d), lambda i: (i, 0)), ng.spec, wg.spec, wu.spec, wd.spec],
        out_specs=pl.BlockSpec((tm, d), lambda i: (i, 0)),
        out_shape=jax.ShapeDtypeStruct((t, d), F32),
        scratch_shapes=[pltpu.VMEM((tm, d), BF16), pltpu.VMEM((tm, d), F32)],
        compiler_params=_params("parallel"),
        name="ffn",
    )(x, ng.arr, wg.arr, wu.arr, wd.arr)


def _ffn_cast_kernel(x_ref, ng_ref, wg_hbm, wu_hbm, wd_hbm, o_ref, wgb_hbm, wub_hbm, wdb_hbm,
                     wg_ref, wu_ref, wd_ref, sg_ref, su_ref, sd_ref, h_ref, acc_ref, sem_in, sem_out,
                     *, layer, half, pre, post):
    n_chunks, _, tf = wg_ref.shape
    n_slots = sg_ref.shape[0]

    def loads(c):
        slot = c % n_slots
        cols = pl.ds(c * tf, tf)
        return (pltpu.make_async_copy(wg_hbm.at[layer, half, :, cols], sg_ref.at[slot], sem_in.at[0, slot]),
                pltpu.make_async_copy(wu_hbm.at[layer, half, :, cols], su_ref.at[slot], sem_in.at[1, slot]),
                pltpu.make_async_copy(wd_hbm.at[layer, half, cols, :], sd_ref.at[slot], sem_in.at[2, slot]))

    def stores(c):
        return (pltpu.make_async_copy(wg_ref.at[c], wgb_hbm.at[c], sem_out.at[0, c]),
                pltpu.make_async_copy(wu_ref.at[c], wub_hbm.at[c], sem_out.at[1, c]),
                pltpu.make_async_copy(wd_ref.at[c], wdb_hbm.at[c], sem_out.at[2, c]))

    def fetch(c):
        slot = c % n_slots
        for cp in loads(c):
            cp.wait()
        wg_ref[c] = sg_ref[slot].astype(BF16)
        wu_ref[c] = su_ref[slot].astype(BF16)
        wd_ref[c] = sd_ref[slot].astype(BF16)
        for cp in stores(c):
            cp.start()
        if c + n_slots < n_chunks:
            for cp in loads(c + n_slots):
                cp.start()

    for c in range(min(n_slots, n_chunks)):
        for cp in loads(c):
            cp.start()
    _ffn_compute(x_ref, ng_ref, wg_ref, wu_ref, wd_ref, o_ref, h_ref, acc_ref, pre=pre, post=post,
                 n_sub=1, before_chunk=fetch)
    for c in range(n_chunks):
        for cp in stores(c):
            cp.wait()


def _ffn_cast(x, ng, wg, wu, wd, layer, half):
    t, d = x.shape
    d_ff = wg.shape[-1]
    tf = FFN_CHUNK
    n_chunks = d_ff // tf
    assert t <= 2 * TOKEN_TILE and d_ff % tf == 0
    kern = functools.partial(_ffn_cast_kernel, layer=layer, half=half, pre=4 * half, post=4 * half + 1)
    hbm = pl.BlockSpec(memory_space=pl.ANY)
    col_chunks = jax.ShapeDtypeStruct((n_chunks, d, tf), BF16)
    row_chunks = jax.ShapeDtypeStruct((n_chunks, tf, d), BF16)
    return pl.pallas_call(
        kern,
        grid=(1,),
        in_specs=[pl.BlockSpec((t, d), lambda i: (0, 0)), ng.spec, hbm, hbm, hbm],
        out_specs=[pl.BlockSpec((t, d), lambda i: (0, 0)), hbm, hbm, hbm],
        out_shape=[jax.ShapeDtypeStruct((t, d), F32), col_chunks, col_chunks, row_chunks],
        scratch_shapes=[pltpu.VMEM(col_chunks.shape, BF16), pltpu.VMEM(col_chunks.shape, BF16),
                        pltpu.VMEM(row_chunks.shape, BF16),
                        pltpu.VMEM((FFN_STAGE_SLOTS, d, tf), F32), pltpu.VMEM((FFN_STAGE_SLOTS, d, tf), F32),
                        pltpu.VMEM((FFN_STAGE_SLOTS, tf, d), F32),
                        pltpu.VMEM((t, d), BF16), pltpu.VMEM((t, d), F32),
                        pltpu.SemaphoreType.DMA((3, FFN_STAGE_SLOTS)),
                        pltpu.SemaphoreType.DMA((3, n_chunks))],
        compiler_params=_params("arbitrary"),
        name="ffn_cast",
    )(x, ng.arr, wg, wu, wd)


def _pool_seq_kernel(x_ref, past_ref, ng_ref, w_ref, sc_ref, o_ref, np_ref, hp_ref, lv_ref,
                     *, ts, pos0, pre, post):
    s = pl.program_id(1)
    base = POOL_PAST + 1
    off = 2 * base
    n = off + ts
    d = x_ref.shape[-1]
    gw = d // len(POOL_WINDOWS)

    @pl.when(s == 0)
    def _():
        hp_ref[0:off - POOL_PAST, :] = jnp.zeros((off - POOL_PAST, d), F32)
        hp_ref[off - POOL_PAST:off, :] = past_ref[0]
        lv_ref[:, 0:base, :] = jnp.zeros((lv_ref.shape[0], base, d), F32)

    x = x_ref[0]
    h = _rms(x, ng_ref[pre:pre + 1, :])
    hp_ref[off:n, :] = h
    pos = pos0 + s * ts + lax.broadcasted_iota(jnp.int32, (ts, 1), 0)
    outs = []
    src = hp_ref
    for gi, w in enumerate(POOL_WINDOWS):
        lo, hi = gi * gw, (gi + 1) * gw
        half = w // 2
        if gi + 1 < len(POOL_WINDOWS):
            dst = lv_ref.at[gi]
            dst[base:n, lo:] = src[base:n, lo:] + src[base - half:n - half, lo:]
            win = dst[off:n, lo:hi]
            src = dst
        else:
            win = src[off:n, lo:hi] + src[off - half:n - half, lo:hi]
        cnt = jnp.minimum(pos + 1, w).astype(F32)
        outs.append(_dot(win / cnt - h[:, lo:hi], w_ref[gi]))
    y = jnp.concatenate(outs, axis=-1) * sc_ref[...]
    o_ref[0] = x + _rms(y, ng_ref[post:post + 1, :])
    tail = hp_ref[n - POOL_PAST:n, :]
    np_ref[0] = tail
    hp_ref[off - POOL_PAST:off, :] = tail


def _pool_seq(x, past, ng, w, sc, pos0, pre, post):
    b, s, d = x.shape
    ts = min(TOKEN_TILE, s)
    assert all(w == 2 << i for i, w in enumerate(POOL_WINDOWS))
    kern = functools.partial(_pool_seq_kernel, ts=ts, pos0=pos0, pre=pre, post=post)
    return pl.pallas_call(
        kern,
        grid=(b, s // ts),
        in_specs=[pl.BlockSpec((1, ts, d), lambda i, j: (i, j, 0)),
                  pl.BlockSpec((1, POOL_PAST, d), lambda i, j: (i, 0, 0)),
                  ng.spec, w.spec, sc.spec],
        out_specs=[pl.BlockSpec((1, ts, d), lambda i, j: (i, j, 0)),
                   pl.BlockSpec((1, POOL_PAST, d), lambda i, j: (i, 0, 0))],
        out_shape=[jax.ShapeDtypeStruct((b, s, d), F32),
                   jax.ShapeDtypeStruct((b, POOL_PAST, d), F32)],
        scratch_shapes=[pltpu.VMEM((2 * (POOL_PAST + 1) + ts, d), F32),
                        pltpu.VMEM((len(POOL_WINDOWS) - 1, 2 * (POOL_PAST + 1) + ts, d), F32)],
        compiler_params=_params("parallel", "arbitrary"),
        name="pool_seq",
    )(x, past, ng.arr, w.arr, sc.arr)


def _pool_step_kernel(x_ref, past_ref, ng_ref, w_ref, sc_ref, o_ref, np_ref, *, pos0, pre, post):
    steps, bb, d = x_ref.shape
    gw = d // len(POOL_WINDOWS)
    xs = [x_ref[t] for t in range(steps)]
    hs = [_rms(x, ng_ref[pre:pre + 1, :]) for x in xs]
    hp = [past_ref[p] for p in range(POOL_PAST)] + hs
    outs = []
    for gi, w in enumerate(POOL_WINDOWS):
        lo, hi = gi * gw, (gi + 1) * gw
        rows = []
        for t in range(steps):
            win = hp[POOL_PAST + t][:, lo:hi]
            for k in range(1, w):
                win = win + hp[POOL_PAST + t - k][:, lo:hi]
            cnt = float(min(pos0 + t + 1, w))
            rows.append(win / cnt - hs[t][:, lo:hi])
        outs.append(_dot(jnp.concatenate(rows, axis=0), w_ref[gi]))
    y = jnp.concatenate(outs, axis=-1) * sc_ref[...]
    for t in range(steps):
        o_ref[t] = xs[t] + _rms(y[t * bb:(t + 1) * bb], ng_ref[post:post + 1, :])
    for p in range(POOL_PAST):
        np_ref[p] = hp[steps + p]


def _pool_step(x, past, ng, w, sc, pos0, pre, post):
    steps, b, d = x.shape
    bb = min(32, b)
    kern = functools.partial(_pool_step_kernel, pos0=pos0, pre=pre, post=post)
    return pl.pallas_call(
        kern,
        grid=(b // bb,),
        in_specs=[pl.BlockSpec((steps, bb, d), lambda i: (0, i, 0)),
                  pl.BlockSpec((POOL_PAST, bb, d), lambda i: (0, i, 0)),
                  ng.spec, w.spec, sc.spec],
        out_specs=[pl.BlockSpec((steps, bb, d), lambda i: (0, i, 0)),
                   pl.BlockSpec((POOL_PAST, bb, d), lambda i: (0, i, 0))],
        out_shape=[jax.ShapeDtypeStruct((steps, b, d), F32),
                   jax.ShapeDtypeStruct((POOL_PAST, b, d), F32)],
        compiler_params=_params("parallel"),
        name="pool_step",
    )(x, past, ng.arr, w.arr, sc.arr)


def _dn_proj_kernel(x_ref, ng_ref, w_ref, wgate_ref, alog_ref, dtb_ref, cp_ref, cw_ref,
                    hm_ref, beta_ref, g_ref, tail_ref, buf_ref, carry_ref,
                    *, pre, tn, n_kh, bb, tiles_per_grp):
    i = pl.program_id(0)
    tm = x_ref.shape[0]
    n_blk = hm_ref.shape[0]
    hv = 2 * n_kh
    n_conv = 4 * n_kh
    n_seg = cp_ref.shape[0]
    seg = tm // n_seg
    halo = CONV_PAST * bb
    off = buf_ref.shape[1] - seg
    per = tn // HEAD_DIM
    carried = tiles_per_grp > 1
    h = _rms(x_ref[...], ng_ref[pre:pre + 1, :]).astype(BF16)

    if carried:
        @pl.when(i % tiles_per_grp == 0)
        def _():
            carry_ref[...] = cp_ref[0]

    def slot_of(c, m, s):
        return ((c % 2) * per + m) * n_seg + s

    def stage(c, res):
        for m in range(per):
            cb = c * per + m
            blk = res[:, m * HEAD_DIM:(m + 1) * HEAD_DIM]
            if cb >= n_conv:
                hm_ref[cb] = blk
                continue
            for s in range(n_seg):
                slot = slot_of(c, m, s)
                buf_ref[slot, off - halo:off, :] = carry_ref[cb] if carried else cp_ref[s, cb]
                buf_ref[slot, off:off + seg, :] = blk[s * seg:(s + 1) * seg, :]
                tail = buf_ref[slot, off + seg - halo:off + seg, :]
                if carried:
                    carry_ref[cb] = tail
                tail_ref[s, cb] = tail

    def convolve(c):
        piece = min(seg, CONV_PIECE)
        for m in range(per):
            cb = c * per + m
            if cb >= n_conv:
                continue
            for s in range(n_seg):
                slot = slot_of(c, m, s)
                for r0 in range(0, seg, piece):
                    acc = None
                    for j in range(CONV_W):
                        lo = off - halo + j * bb + r0
                        term = buf_ref[slot, lo:lo + piece, :] * cw_ref[cb, j:j + 1, :]
                        acc = term if acc is None else acc + term
                    out = _silu(acc)
                    if cb < 2 * n_kh:
                        out = out * lax.rsqrt(jnp.sum(out * out, axis=-1, keepdims=True) + EPS)
                        if cb < n_kh:
                            out = out * (HEAD_DIM ** -0.5)
                    hm_ref[cb, s * seg + r0:s * seg + r0 + piece, :] = out

    n_chunks = n_blk // per
    for c in range(n_chunks):
        res = jnp.dot(h, w_ref[:, c * tn:(c + 1) * tn], preferred_element_type=F32)
        if c > 0:
            convolve(c - 1)
        stage(c, res)
    convolve(n_chunks - 1)
    gates = jnp.dot(h, wgate_ref[...], preferred_element_type=F32)
    b = gates[:, :hv]
    a = gates[:, hv:] + dtb_ref[...]
    beta_ref[...] = 1.0 / (1.0 + jnp.exp(-b))
    softplus = jnp.maximum(a, 0.0) + jnp.log1p(jnp.exp(-jnp.abs(a)))
    g_ref[...] = -jnp.exp(alog_ref[...]) * softplus


def _dn_proj(x, ng, w_main, w_gates, a_log, dt_bias, conv_past, conv_w, bb, pre):
    t, d = x.shape
    tm = min(TOKEN_TILE, t)
    hv = a_log.arr.shape[-1]
    n_blk = w_main.arr.shape[-1] // HEAD_DIM
    n_grp, n_conv, halo, dk = conv_past.shape
    rows_per_grp = t // n_grp
    n_seg = max(1, tm // rows_per_grp)
    tiles_per_grp = max(1, rows_per_grp // tm)
    assert (rows_per_grp % tm == 0 or tm % rows_per_grp == 0) and tm // n_seg >= halo
    tn = 4 * HEAD_DIM
    kern = functools.partial(_dn_proj_kernel, pre=pre, tn=tn, n_kh=hv // 2, bb=bb,
                             tiles_per_grp=tiles_per_grp)
    row = lambda n: pl.BlockSpec((tm, n), lambda i: (i, 0))
    past = pl.BlockSpec((n_seg, n_conv, halo, dk), lambda i: (i // tiles_per_grp, 0, 0, 0))
    off = -(-halo // 8) * 8
    return pl.pallas_call(
        kern,
        grid=(t // tm,),
        in_specs=[row(d), ng.spec, w_main.spec, w_gates.spec, a_log.spec, dt_bias.spec, past, conv_w.spec],
        out_specs=[pl.BlockSpec((n_blk, tm, HEAD_DIM), lambda i: (0, i, 0)), row(hv), row(hv), past],
        out_shape=[jax.ShapeDtypeStruct((n_blk, t, HEAD_DIM), F32),
                   jax.ShapeDtypeStruct((t, hv), F32), jax.ShapeDtypeStruct((t, hv), F32),
                   jax.ShapeDtypeStruct(conv_past.shape, F32)],
        scratch_shapes=[pltpu.VMEM((2 * (tn // HEAD_DIM) * n_seg, off + tm // n_seg, dk), F32),
                        pltpu.VMEM((n_conv, halo, dk), F32)],
        compiler_params=_params("arbitrary"),
        name="dn_proj",
    )(x, ng.arr, w_main.arr, w_gates.arr, a_log.arr, dt_bias.arr, conv_past, conv_w.arr)


def _dot3_each(pairs):
    split = lambda x: (x.astype(BF16), (x - x.astype(BF16).astype(F32)).astype(BF16))
    parts = [(split(a), split(b)) for a, b in pairs]
    mm = lambda x, y: jnp.dot(x, y, preferred_element_type=F32)
    hh = [mm(a[0], b[0]) for a, b in parts]
    hl = [mm(a[0], b[1]) for a, b in parts]
    lh = [mm(a[1], b[0]) for a, b in parts]
    return [x + (y + z) for x, y, z in zip(hh, hl, lh)]


def _unit_lower_inverse_each(l_mats, c):
    row = lax.broadcasted_iota(jnp.int32, (c, c), 0)
    col = lax.broadcasted_iota(jnp.int32, (c, c), 1)
    blk = jnp.bitwise_xor(row, col)
    eye = jnp.where(row == col, 1.0, 0.0)
    bs = min(INV_BASE, c)
    ps = [jnp.where(blk < bs, -l, 0.0) for l in l_mats]
    ts = [eye + p for p in ps]
    n = 2
    while n < bs:
        ps = [_dot(p, p) for p in ps]
        tps = [_dot(t, p) for t, p in zip(ts, ps)]
        ts = [t + tp for t, tp in zip(ts, tps)]
        n *= 2
    while bs < c:
        es = [jnp.where((blk < 2 * bs) & (blk >= bs), l, 0.0) for l in l_mats]
        tes = [_dot(t, e) for t, e in zip(ts, es)]
        tets = [_dot(te, t) for te, t in zip(tes, ts)]
        ts = [t - tet for t, tet in zip(ts, tets)]
        bs *= 2
    lts = _dot3_each(list(zip(l_mats, ts)))
    trs = [_dot(t, eye - t - lt) for t, lt in zip(ts, lts)]
    return [t + tr for t, tr in zip(ts, trs)]


def _dn_seq_kernel(stack_ref, hm_ref, beta_ref, g_ref, s0_ref, nw_ref,
                   o_ref, sn_ref,
                   u_ref, wq_ref, attn_ref, kdt_ref, egl_ref, *, c, bb, n_kh):
    step = pl.program_id(1)
    dk = HEAD_DIM
    hv = 2 * n_kh
    n_conv = 4 * n_kh
    rows_per_step = hm_ref.shape[1]
    n_chunks = rows_per_step // c

    @pl.when(step == 0)
    def _():
        sn_ref[...] = s0_ref[...]

    row = lax.broadcasted_iota(jnp.int32, (c, c), 0)
    col = lax.broadcasted_iota(jnp.int32, (c, c), 1)
    same = (jnp.bitwise_xor(row, col) & (bb - 1)) == 0
    low_eq = same & (row >= col)
    low = same & (row > col)
    up_eq = same & (row <= col)
    last = same & (col >= c - bb)
    eye = row == col
    head_lane = lax.broadcasted_iota(jnp.int32, (1, hv), 1)
    member = lax.broadcasted_iota(jnp.int32, (c, 1), 0) & (bb - 1)


    def intra(units):
        qs = [hm_ref[hk, ci * c:(ci + 1) * c, :] for hk, ci in units]
        ks = [hm_ref[n_kh + hk, ci * c:(ci + 1) * c, :] for hk, ci in units]
        kks = [_dot_nt(k, k) for k in ks]
        qks = [_dot_nt(q, k) for q, k in zip(qs, ks)]
        probs = []
        for (hk, ci), q, k, kk, qk in zip(units, qs, ks, kks, qks):
            r0 = ci * c
            beta_blk = beta_ref[r0:r0 + c, :]
            g_blk = g_ref[r0:r0 + c, :]
            for j in range(2):
                h = 2 * hk + j
                sel = head_lane == h
                beta = jnp.sum(jnp.where(sel, beta_blk, 0.0), axis=1, keepdims=True)
                g = jnp.sum(jnp.where(sel, g_blk, 0.0), axis=1, keepdims=True)
                g_row = jnp.sum(jnp.where(eye, g, 0.0), axis=0, keepdims=True)
                gc = jnp.sum(jnp.where(low_eq, g_row, 0.0), axis=1, keepdims=True)
                gc_row = jnp.sum(jnp.where(up_eq, g, 0.0), axis=0, keepdims=True)
                g_last = jnp.sum(jnp.where(last, gc_row, 0.0), axis=1, keepdims=True)
                decay = jnp.exp(jnp.where(low_eq, gc - gc_row, -jnp.inf))
                egc = jnp.exp(gc)
                attn_ref[h, r0:r0 + c, :] = (qk * decay).astype(BF16)
                wq_ref[h, ci, c:2 * c, :] = (q * egc).astype(BF16)
                kdt_ref[h, ci] = (k * jnp.exp(g_last - gc)).T.astype(BF16)
                egl_ref[h, ci] = jnp.broadcast_to(egc[c - bb:c, :], (bb, dk))
                v = hm_ref[2 * n_kh + h, r0:r0 + c, :]
                rhs = jnp.concatenate([v * beta, k * (beta * egc)], axis=1)
                probs.append((h, ci, jnp.where(low, kk * beta * decay, 0.0), rhs))
        t_mats = _unit_lower_inverse_each([p[2] for p in probs], c)
        uws = [_dot(t, p[3]) for t, p in zip(t_mats, probs)]
        for (h, ci, _, _), uw in zip(probs, uws):
            u_ref[h, ci * c:(ci + 1) * c, :] = uw[:, :dk]
            wq_ref[h, ci, 0:c, :] = uw[:, dk:].astype(BF16)

    kh_per_iter = min(n_kh, max(1, INTRA_UNITS // n_chunks))

    def intra_body(it, carry):
        intra([(it * kh_per_iter + m, ci) for m in range(kh_per_iter) for ci in range(n_chunks)])
        return carry

    lax.fori_loop(0, n_kh // kh_per_iter, intra_body, 0)

    def inter(ci, heads):
        rows = pl.ds(pl.multiple_of(ci * c, c), c)
        mm = lambda x, y: jnp.dot(x, y, preferred_element_type=F32)
        res = [[mm(wq_ref[h, ci], sn_ref[s, h].astype(BF16)) for s in range(bb)] for h in heads]
        v_news, o_states = [], []
        for h, res_h in zip(heads, res):
            v_new = u_ref[h, rows, :]
            o_state = None
            for s, r in enumerate(res_h):
                if bb == 1:
                    v_new = v_new - r[:c]
                    o_state = r[c:]
                else:
                    mine = member == s
                    v_new = v_new - jnp.where(mine, r[:c], 0.0)
                    o_state = jnp.where(mine, r[c:], 0.0 if o_state is None else o_state)
            v_news.append(v_new)
            o_states.append(o_state)
        v_seq = [[(v if bb == 1 else jnp.where(member == s, v, 0.0)).astype(BF16) for s in range(bb)]
                 for v in v_news]
        attn_v = [mm(attn_ref[h, rows, :], v.astype(BF16)) for h, v in zip(heads, v_news)]
        upd = [[mm(kdt_ref[h, ci], v_s) for v_s in v_h] for h, v_h in zip(heads, v_seq)]
        for h, o_state, av, upd_h in zip(heads, o_states, attn_v, upd):
            for s in range(bb):
                sn_ref[s, h] = sn_ref[s, h] * egl_ref[h, ci, s:s + 1, :] + upd_h[s]
            z = hm_ref[n_conv + h, rows, :]
            o_ref[h, rows, :] = (_rms(o_state + av, nw_ref[...]) * _silu(z)).astype(BF16)

    heads_per_iter = min(hv, max(1, INTER_PROBLEMS // bb))

    def inter_body(ci, carry):
        if heads_per_iter == hv:
            inter(ci, list(range(hv)))
        else:
            def head_body(it, carry2):
                inter(ci, [it * heads_per_iter + m for m in range(heads_per_iter)])
                return carry2
            lax.fori_loop(0, hv // heads_per_iter, head_body, 0)
        return carry

    lax.fori_loop(0, n_chunks, inter_body, 0)


def _dn_seq(hm, beta, g, s0_all, layer, stack, norm_w, c, bb, rows_per_step):
    n_blk, t, dk = hm.shape
    n_grp = s0_all.shape[1] // bb
    hv = s0_all.shape[2]
    n_kh = hv // 2
    n_conv = 4 * n_kh
    rows_per_grp = t // n_grp
    steps = rows_per_grp // rows_per_step
    n_chunks = rows_per_step // c
    assert n_blk == n_conv + hv and rows_per_step % c == 0 and rows_per_grp % rows_per_step == 0
    assert bb & (bb - 1) == 0 and c & (c - 1) == 0 and c % bb == 0
    kern = functools.partial(_dn_seq_kernel, c=c, bb=bb, n_kh=n_kh)
    tok = lambda i, j: i * steps + j
    state = pl.BlockSpec((None, bb, hv, dk, dk), lambda i, j: (layer, i, 0, 0, 0))
    gate = pl.BlockSpec((rows_per_step, hv), lambda i, j: (tok(i, j), 0))
    return pl.pallas_call(
        kern,
        grid=(n_grp, steps),
        in_specs=[pl.BlockSpec(memory_space=pl.ANY),
                  pl.BlockSpec((n_blk, rows_per_step, dk), lambda i, j: (0, tok(i, j), 0)),
                  gate, gate, state, norm_w.spec],
        out_specs=[pl.BlockSpec((hv, rows_per_step, dk), lambda i, j: (0, tok(i, j), 0)), state],
        out_shape=[jax.ShapeDtypeStruct((hv, t, dk), BF16), jax.ShapeDtypeStruct(s0_all.shape, F32)],
        scratch_shapes=[pltpu.VMEM((hv, rows_per_step, dk), F32),
                        pltpu.VMEM((hv, n_chunks, 2 * c, dk), BF16),
                        pltpu.VMEM((hv, rows_per_step, c), BF16),
                        pltpu.VMEM((hv, n_chunks, dk, c), BF16),
                        pltpu.VMEM((hv, n_chunks, bb, dk), F32)],
        input_output_aliases={} if stack is None else {0: 1},
        compiler_params=_params("parallel", "arbitrary"),
        name="dn_seq",
    )(s0_all if stack is None else stack, hm, beta, g, s0_all, norm_w.arr)


def _dn_out_kernel(o_ref, x_ref, ng_ref, w_ref, y_ref, *, post):
    o = jnp.concatenate([o_ref[h] for h in range(o_ref.shape[0])], axis=1)
    y = _dot(o, w_ref[...])
    y_ref[...] = x_ref[...] + _rms(y, ng_ref[post:post + 1, :])


def _dn_out(o, x, ng, w_out, post):
    t, d = x.shape
    hv, _, dk = o.shape
    tm = min(TOKEN_TILE, t)
    kern = functools.partial(_dn_out_kernel, post=post)
    return pl.pallas_call(
        kern,
        grid=(t // tm,),
        in_specs=[pl.BlockSpec((hv, tm, dk), lambda i: (0, i, 0)),
                  pl.BlockSpec((tm, d), lambda i: (i, 0)), ng.spec, w_out.spec],
        out_specs=pl.BlockSpec((tm, d), lambda i: (i, 0)),
        out_shape=jax.ShapeDtypeStruct((t, d), F32),
        compiler_params=_params("parallel"),
        name="dn_out",
    )(o, x, ng.arr, w_out.arr)


def _dn_layer(x, conv_past, s0_all, layer, stack, ng, w_main, w_gates, a_log, dt_bias, conv_w, norm_w,
              w_out):
    b, s, d = x.shape
    dk = HEAD_DIM
    conv_dim = conv_past.shape[-1]
    n_conv = conv_dim // dk
    assert s >= CONV_PAST
    if s >= CHUNK:
        bb, c, rows_per_step = 1, CHUNK, min(s, 4 * CHUNK)
    else:
        bb = min(SHORT_GROUP, b)
        c = rows_per_step = s * bb
    n_grp = b // bb
    xt = x.reshape(n_grp, bb, s, d).transpose(0, 2, 1, 3).reshape(b * s, d)
    cp = conv_past.reshape(n_grp, bb, CONV_PAST, n_conv, dk).transpose(0, 3, 2, 1, 4)
    cp = cp.reshape(n_grp, n_conv, CONV_PAST * bb, dk)
    hm, beta, g, tail = _dn_proj(xt, ng, w_main, w_gates, a_log, dt_bias, cp, conv_w, bb, pre=2)
    o, stack = _dn_seq(hm, beta, g, s0_all, layer, stack, norm_w, c, bb, rows_per_step)
    y = _dn_out(o, xt, ng, w_out, post=3)
    y = y.reshape(n_grp, s, bb, d).transpose(0, 2, 1, 3).reshape(b, s, d)
    new_conv = tail.reshape(n_grp, n_conv, CONV_PAST, bb, dk).transpose(0, 3, 2, 1, 4)
    return y, new_conv.reshape(b, CONV_PAST, conv_dim), stack


def _trunk(x, st_pool, st_conv, st_delta, pos0, p, ffn):
    b, s, d = x.shape
    depth = p["ng"].shape[0]
    new_pool, new_conv, new_delta = [], [], None
    for i in range(depth):
        ng = _Res(p["ng"], (i,))
        x = ffn(x.reshape(b * s, d), ng, i, 0).reshape(b, s, d)
        j = i // 2
        if i % 2 == 0:
            pool_w, pool_sc = _Res(p["pool_w"], (j,)), _Res(p["pool_scale"], (j,))
            if s >= 2 * POOL_PAST:
                x, ps = _pool_seq(x, st_pool[j], ng, pool_w, pool_sc, pos0, 2, 3)
            else:
                xt, pt = _pool_step(x.transpose(1, 0, 2), st_pool[j].transpose(1, 0, 2), ng,
                                    pool_w, pool_sc, pos0, 2, 3)
                x, ps = xt.transpose(1, 0, 2), pt.transpose(1, 0, 2)
            new_pool.append(ps)
        else:
            x, cs, new_delta = _dn_layer(
                x, st_conv[j], st_delta, j, new_delta, ng, _Res(p["w_main"], (j,)),
                _Res(p["w_gates"], (j,)), _Res(p["a_log"], (j,)), _Res(p["dt_bias"], (j,)), _Res(p["conv_w"], (j,)),
                _Res(p["norm_w"], (j,)), _Res(p["w_out"], (j,)))
            new_conv.append(cs)
        x = ffn(x.reshape(b * s, d), ng, i, 1).reshape(b, s, d)
    return x, jnp.stack(new_pool), jnp.stack(new_conv), new_delta


def kernel(x_prompt, x_sample, state_pool, state_conv, state_delta, norm_gains, w_ffn_gate,
           w_ffn_up, w_ffn_down, pool_w, pool_scale, dn_w_in, dn_conv_w, dn_a_log, dn_dt_bias,
           dn_norm_w, dn_w_out):
    n_dn, _, _, conv_dim = state_conv.shape
    hv = state_delta.shape[2]
    n_pool = state_pool.shape[0]
    d = x_prompt.shape[-1]
    n_conv = conv_dim // HEAD_DIM
    n_blk = n_conv + hv
    p = {
        "ng": norm_gains,
        "pool_w": pool_w.astype(BF16), "pool_scale": pool_scale.reshape(n_pool, 1, d),
        "w_main": dn_w_in[:, :, :n_blk * HEAD_DIM].astype(BF16),
        "w_gates": dn_w_in[:, :, n_blk * HEAD_DIM:].astype(BF16),
        "a_log": dn_a_log.reshape(n_dn, 1, hv), "dt_bias": dn_dt_bias.reshape(n_dn, 1, hv),
        "conv_w": dn_conv_w.reshape(n_dn, CONV_W, n_conv, HEAD_DIM).transpose(0, 2, 1, 3),
        "norm_w": dn_norm_w.reshape(n_dn, 1, HEAD_DIM),
        "w_out": dn_w_out.astype(BF16),
    }
    bp = x_prompt.shape[0]
    zp = jnp.zeros((n_pool, bp) + state_pool.shape[2:], F32)
    zc = jnp.zeros((n_dn, bp) + state_conv.shape[2:], F32)
    zd = jnp.zeros((n_dn, bp) + state_delta.shape[2:], F32)
    bf16_w = {}

    def ffn_sample(x2, ng, layer, half):
        y, *bf16_w[layer, half] = _ffn_cast(x2, ng, w_ffn_gate, w_ffn_up, w_ffn_down, layer, half)
        return y

    def ffn_prompt(x2, ng, layer, half):
        return _ffn(x2, ng, *(_Res(w) for w in bf16_w[layer, half]), half)

    y_s, pool_s, conv_s, delta_s = _trunk(x_sample, state_pool, state_conv, state_delta, PAST_LEN, p, ffn_sample)
    y_p, pool_p, conv_p, delta_p = _trunk(x_prompt, zp, zc, zd, 0, p, ffn_prompt)
    return (y_p, y_s, pool_p, conv_p, delta_p, pool_s, conv_s, delta_s)
```
